```python
import jax, jax.numpy as jnp
from jax import lax
import numpy as np

D_MODEL = 1024
BATCH = 8
SEQ = 2048
DEPTH = 2
DEC_BATCH = 128
DEC_SEQ = 4
PAST_LEN = 2048
PAGE_SIZE = 128

N_MIXERS = 2
N_CONV_LAYERS = (DEPTH + 1) // 2
N_SB_LAYERS = DEPTH // 2
CONV_WIDTH = 31
N_HEADS = 16
HEAD_DIM = D_MODEL // N_HEADS
Q_BLOCK = 128
N_GROUPS = 4
EXPERTS_PER_GROUP = 8
N_EXPERTS = N_GROUPS * EXPERTS_PER_GROUP
TOP_K_FINE = 2
D_EXPERT = D_MODEL // 4
NORM_EPS = 1e-6
LN_EPS = 1e-5
SB_BIAS_INIT = -6.0

kernel_name = "hybrid_conformer_conv_stickbreaking_hmoe_step"


def rmsnorm(x, g):
    xf = x.astype(jnp.float32)
    y = xf * lax.rsqrt(jnp.mean(xf * xf, axis=-1, keepdims=True) + NORM_EPS)
    return (y * g.astype(jnp.float32)).astype(x.dtype)


def layernorm(x, g, b):
    xf = x.astype(jnp.float32)
    mu = jnp.mean(xf, axis=-1, keepdims=True)
    xc = xf - mu
    var = jnp.mean(xc * xc, axis=-1, keepdims=True)
    y = xc * lax.rsqrt(var + LN_EPS) * g.astype(jnp.float32) + b.astype(jnp.float32)
    return y.astype(x.dtype)


def conv_module(x, hist, w_pw1, b_pw1, w_dw, b_dw, ln_g, ln_b, w_pw2, b_pw2):
    a = x @ w_pw1 + b_pw1
    u = a[..., :D_MODEL] * jax.nn.sigmoid(a[..., D_MODEL:])
    full = jnp.concatenate([hist.astype(u.dtype), u], axis=1)
    c = lax.conv_general_dilated(full, w_dw[:, None, :], window_strides=(1,), padding='VALID',
                                 dimension_numbers=('NWC', 'WIO', 'NWC'),
                                 feature_group_count=D_MODEL) + b_dw
    c = jax.nn.silu(layernorm(c, ln_g, ln_b))
    y = c @ w_pw2 + b_pw2
    return y, full[:, -(CONV_WIDTH - 1):]


def stick_breaking(q, k, v, q_pos, k_pos, bias):
    z = jnp.einsum('bqhd,bkhd->bhqk', q, k).astype(jnp.float32) * (HEAD_DIM ** -0.5)
    z = z + bias.astype(jnp.float32)[None, :, None, None]
    mask = k_pos[None, :] < q_pos[:, None]
    log_fail = jnp.where(mask, jax.nn.log_sigmoid(-z), 0.0)
    between = lax.cumsum(log_fail, axis=3, reverse=True) - log_fail
    attn = jnp.where(mask, jnp.exp(jax.nn.log_sigmoid(z) + between), 0.0)
    return jnp.einsum('bhqk,bkhd->bqhd', attn.astype(v.dtype), v)


def qkv_proj(x, w_qkv):
    B, T, _ = x.shape
    qkv = (x @ w_qkv).reshape(B, T, 3, N_HEADS, HEAD_DIM)
    return qkv[:, :, 0], qkv[:, :, 1], qkv[:, :, 2]


def sb_prompt(x, w_qkv, w_o, bias):
    B, T, _ = x.shape
    q, k, v = qkv_proj(x, w_qkv)
    nb = T // Q_BLOCK
    pos = jnp.arange(T, dtype=jnp.int32)
    qb = q.reshape(B, nb, Q_BLOCK, N_HEADS, HEAD_DIM).transpose(1, 0, 2, 3, 4)
    pb = pos.reshape(nb, Q_BLOCK)
    ob = lax.map(lambda a: stick_breaking(a[0], k, v, a[1], pos, bias), (qb, pb))
    o = ob.transpose(1, 0, 2, 3, 4).reshape(B, T, D_MODEL)
    return o @ w_o, k, v


def sb_sample(x, cache_k, cache_v, page_table, w_qkv, w_o, bias):
    B, T, _ = x.shape
    q, k, v = qkv_proj(x, w_qkv)
    past = page_table.shape[1] * PAGE_SIZE
    k_past = cache_k[page_table].reshape(B, past, N_HEADS, HEAD_DIM)
    v_past = cache_v[page_table].reshape(B, past, N_HEADS, HEAD_DIM)
    k_all = jnp.concatenate([k_past.astype(k.dtype), k], axis=1)
    v_all = jnp.concatenate([v_past.astype(v.dtype), v], axis=1)
    q_pos = past + jnp.arange(T, dtype=jnp.int32)
    k_pos = jnp.arange(past + T, dtype=jnp.int32)
    o = stick_breaking(q, k_all, v_all, q_pos, k_pos, bias).reshape(B, T, D_MODEL)
    return o @ w_o, k, v


def hier_moe(x, w_coarse, b_coarse, w_fine, b_fine, w_gate, w_up, w_down):
    B, T, _ = x.shape
    xt = x.reshape(B * T, D_MODEL)
    coarse = (xt @ w_coarse).astype(jnp.float32) + b_coarse.astype(jnp.float32)
    p_group = jax.nn.softmax(coarse, axis=-1)
    g_top, g_idx = lax.top_k(p_group, 1)
    fine_all = jnp.einsum('nd,gde->nge', xt, w_fine).astype(jnp.float32) + b_fine.astype(jnp.float32)
    fine = jnp.take_along_axis(fine_all, g_idx[:, :, None], axis=1)[:, 0]
    p_exp = jax.nn.softmax(fine, axis=-1)
    e_top, e_idx = lax.top_k(p_exp, TOP_K_FINE)
    e_w = e_top / jnp.sum(e_top, axis=-1, keepdims=True)
    within = jnp.sum(jax.nn.one_hot(e_idx, EXPERTS_PER_GROUP, dtype=jnp.float32) * e_w[..., None], axis=1)
    comb = jax.nn.one_hot(g_idx[:, 0], N_GROUPS, dtype=jnp.float32)[:, :, None] * (g_top * within)[:, None, :]
    comb = comb.reshape(B * T, N_EXPERTS).astype(x.dtype)
    h = jax.nn.silu(jnp.einsum('nd,edf->nef', xt, w_gate)) * jnp.einsum('nd,edf->nef', xt, w_up)
    y = jnp.einsum('nef,efd->nd', h * comb[:, :, None], w_down)
    return y.reshape(B, T, D_MODEL)


def setup_inputs(seed: int = 0) -> dict:
    key = jax.random.key(seed)
    ks = jax.random.split(key, 32)
    f32 = jnp.float32

    def nrm(k, shape, scale):
        return jax.random.normal(k, shape, f32) * scale

    n_pages = PAST_LEN // PAGE_SIZE
    n_used = DEC_BATCH * n_pages
    n_pool = n_used + n_used // 4
    perm = jax.random.permutation(ks[5], n_pool)
    page_table = perm[:n_used].reshape(DEC_BATCH, n_pages).astype(jnp.int32)
    D = D_MODEL
    return {
        "x_prompt": nrm(ks[0], (BATCH, SEQ, D), 1.0),
        "x_sample": nrm(ks[1], (DEC_BATCH, DEC_SEQ, D), 1.0),
        "state_conv": nrm(ks[2], (N_CONV_LAYERS, DEC_BATCH, CONV_WIDTH - 1, D), 1.0),
        "cache_k": nrm(ks[3], (N_SB_LAYERS, n_pool, PAGE_SIZE, N_HEADS, HEAD_DIM), 1.0),
        "cache_v": nrm(ks[4], (N_SB_LAYERS, n_pool, PAGE_SIZE, N_HEADS, HEAD_DIM), 1.0),
        "page_table": page_table,
        "norm_mix": 1.0 + nrm(ks[6], (DEPTH, D), 0.05),
        "norm_ffn": 1.0 + nrm(ks[7], (DEPTH, D), 0.05),
        "norm_final": 1.0 + nrm(ks[8], (D,), 0.05),
        "conv_w_pw1": nrm(ks[9], (N_CONV_LAYERS, D, 2 * D), D ** -0.5),
        "conv_b_pw1": nrm(ks[10], (N_CONV_LAYERS, 2 * D), 0.02),
        "conv_w_dw": nrm(ks[11], (N_CONV_LAYERS, CONV_WIDTH, D), CONV_WIDTH ** -0.5),
        "conv_b_dw": nrm(ks[12], (N_CONV_LAYERS, D), 0.02),
        "conv_ln_g": 1.0 + nrm(ks[13], (N_CONV_LAYERS, D), 0.05),
        "conv_ln_b": nrm(ks[14], (N_CONV_LAYERS, D), 0.02),
        "conv_w_pw2": nrm(ks[15], (N_CONV_LAYERS, D, D), D ** -0.5),
        "conv_b_pw2": nrm(ks[16], (N_CONV_LAYERS, D), 0.02),
        "sb_w_qkv": nrm(ks[17], (N_SB_LAYERS, D, 3 * D), D ** -0.5),
        "sb_w_o": nrm(ks[18], (N_SB_LAYERS, D, D), D ** -0.5),
        "sb_logit_bias": SB_BIAS_INIT + nrm(ks[26], (N_SB_LAYERS, N_HEADS), 0.1),
        "moe_w_coarse": nrm(ks[19], (DEPTH, D, N_GROUPS), D ** -0.5),
        "moe_b_coarse": nrm(ks[20], (DEPTH, N_GROUPS), 0.01),
        "moe_w_fine": nrm(ks[21], (DEPTH, N_GROUPS, D, EXPERTS_PER_GROUP), D ** -0.5),
        "moe_b_fine": nrm(ks[22], (DEPTH, N_GROUPS, EXPERTS_PER_GROUP), 0.01),
        "moe_w_gate": nrm(ks[23], (DEPTH, N_EXPERTS, D, D_EXPERT), D ** -0.5),
        "moe_w_up": nrm(ks[24], (DEPTH, N_EXPERTS, D, D_EXPERT), D ** -0.5),
        "moe_w_down": nrm(ks[25], (DEPTH, N_EXPERTS, D_EXPERT, D), D_EXPERT ** -0.5),
    }


def reference(x_prompt, x_sample, state_conv, cache_k, cache_v, page_table, norm_mix, norm_ffn, norm_final,
              conv_w_pw1, conv_b_pw1, conv_w_dw, conv_b_dw, conv_ln_g, conv_ln_b, conv_w_pw2, conv_b_pw2,
              sb_w_qkv, sb_w_o, sb_logit_bias, moe_w_coarse, moe_b_coarse, moe_w_fine, moe_b_fine,
              moe_w_gate, moe_w_up, moe_w_down):
    yp, ys = x_prompt, x_sample
    conv_p, conv_s, k_p, v_p, k_s, v_s = [], [], [], [], [], []
    for i in range(DEPTH):
        j = i // N_MIXERS
        hp = rmsnorm(yp, norm_mix[i])
        hs = rmsnorm(ys, norm_mix[i])
        if i % N_MIXERS == 0:
            cw = (conv_w_pw1[j], conv_b_pw1[j], conv_w_dw[j], conv_b_dw[j],
                  conv_ln_g[j], conv_ln_b[j], conv_w_pw2[j], conv_b_pw2[j])
            zero_hist = jnp.zeros((hp.shape[0], CONV_WIDTH - 1, D_MODEL), hp.dtype)
            mp, sp = conv_module(hp, zero_hist, *cw)
            ms, ss = conv_module(hs, state_conv[j], *cw)
            conv_p.append(sp)
            conv_s.append(ss)
        else:
            mp, kp_new, vp_new = sb_prompt(hp, sb_w_qkv[j], sb_w_o[j], sb_logit_bias[j])
            ms, ks_new, vs_new = sb_sample(hs, cache_k[j], cache_v[j], page_table, sb_w_qkv[j], sb_w_o[j],
                                           sb_logit_bias[j])
            k_p.append(kp_new)
            v_p.append(vp_new)
            k_s.append(ks_new)
            v_s.append(vs_new)
        yp = yp + mp
        ys = ys + ms
        mw = (moe_w_coarse[i], moe_b_coarse[i], moe_w_fine[i], moe_b_fine[i],
              moe_w_gate[i], moe_w_up[i], moe_w_down[i])
        yp = yp + hier_moe(rmsnorm(yp, norm_ffn[i]), *mw)
        ys = ys + hier_moe(rmsnorm(ys, norm_ffn[i]), *mw)
    yp = rmsnorm(yp, norm_final)
    ys = rmsnorm(ys, norm_final)
    return (yp, ys, jnp.stack(conv_p), jnp.stack(conv_s), jnp.stack(k_p), jnp.stack(v_p),
            jnp.stack(k_s), jnp.stack(v_s))
```

```python
import functools

import jax
import jax.numpy as jnp
from jax import lax
from jax.experimental import pallas as pl
from jax.experimental.pallas import tpu as pltpu

D_MODEL = 1024
CONV_WIDTH = 31
N_HEADS = 16
HEAD_DIM = D_MODEL // N_HEADS
N_GROUPS = 4
EXPERTS_PER_GROUP = 8
N_EXPERTS = N_GROUPS * EXPERTS_PER_GROUP
D_EXPERT = D_MODEL // 4
PAGE_SIZE = 128
NORM_EPS = 1e-6
LN_EPS = 1e-5

LANES = 128
SUBLANES = 8
HIST_PAD = 32
VMEM_LIMIT = 56 * 1024 * 1024

F32 = jnp.float32
BF16 = jnp.bfloat16


def _cparams(*sem):
    return pltpu.CompilerParams(dimension_semantics=sem, vmem_limit_bytes=VMEM_LIMIT)


def _rms(x, g):
    return x * lax.rsqrt(jnp.mean(x * x, axis=-1, keepdims=True) + NORM_EPS) * g


def _softplus(z):
    return jnp.maximum(z, 0.0) + jnp.log(1.0 + jnp.exp(-jnp.abs(z)))


def _pw1_glu_kernel(x_ref, g_ref, w_ref, b_ref, u_ref):
    h = _rms(x_ref[...], g_ref[...]).astype(BF16)
    a = jnp.dot(h, w_ref[...], preferred_element_type=F32) + b_ref[...]
    u_ref[...] = a[:, :D_MODEL] * jax.nn.sigmoid(a[:, D_MODEL:])


def _pw1_glu(x, g, w_bf, b, tm):
    n = x.shape[0]
    return pl.pallas_call(
        _pw1_glu_kernel,
        grid=(n // tm,),
        in_specs=[
            pl.BlockSpec((tm, D_MODEL), lambda i: (i, 0)),
            pl.BlockSpec((1, D_MODEL), lambda i: (0, 0)),
            pl.BlockSpec((D_MODEL, 2 * D_MODEL), lambda i: (0, 0)),
            pl.BlockSpec((1, 2 * D_MODEL), lambda i: (0, 0)),
        ],
        out_specs=pl.BlockSpec((tm, D_MODEL), lambda i: (i, 0)),
        out_shape=jax.ShapeDtypeStruct((n, D_MODEL), F32),
        compiler_params=_cparams("arbitrary"),
        name="pw1_glu",
    )(x, g, w_bf, b)


def _qkv_kernel(x_ref, g_ref, w_ref, q_ref, k_ref, v_ref, kb_ref, vb_ref):
    h = _rms(x_ref[...], g_ref[...]).astype(BF16)
    a = jnp.dot(h, w_ref[...], preferred_element_type=F32)
    k = a[:, D_MODEL:2 * D_MODEL]
    v = a[:, 2 * D_MODEL:]
    q_ref[...] = (a[:, :D_MODEL] * (HEAD_DIM ** -0.5)).astype(BF16)
    k_ref[...] = k
    v_ref[...] = v
    kb_ref[...] = k.astype(BF16)
    vb_ref[...] = v.astype(BF16)


def _qkv(x, g, w_bf, tm):
    n = x.shape[0]
    row = pl.BlockSpec((tm, D_MODEL), lambda i: (i, 0))
    return pl.pallas_call(
        _qkv_kernel,
        grid=(n // tm,),
        in_specs=[
            row,
            pl.BlockSpec((1, D_MODEL), lambda i: (0, 0)),
            pl.BlockSpec((D_MODEL, 3 * D_MODEL), lambda i: (0, 0)),
        ],
        out_specs=[row, row, row, row, row],
        out_shape=[
            jax.ShapeDtypeStruct((n, D_MODEL), BF16),
            jax.ShapeDtypeStruct((n, D_MODEL), F32),
            jax.ShapeDtypeStruct((n, D_MODEL), F32),
            jax.ShapeDtypeStruct((n, D_MODEL), BF16),
            jax.ShapeDtypeStruct((n, D_MODEL), BF16),
        ],
        compiler_params=_cparams("arbitrary"),
        name="qkv",
    )(x, g, w_bf)


CONV_ROWS = 16


def _conv_prompt_kernel(u_ref, w_ref, b_ref, c_ref, ext_ref):
    tt = u_ref.shape[1]
    t = pl.program_id(1)

    @pl.when(t == 0)
    def _():
        ext_ref[0:HIST_PAD, :] = jnp.zeros((HIST_PAD, D_MODEL), F32)

    ext_ref[HIST_PAD:HIST_PAD + tt, :] = u_ref[0]
    first = HIST_PAD - (CONV_WIDTH - 1)
    bias = b_ref[...]

    for r0 in range(0, tt, CONV_ROWS):
        acc = jnp.zeros((CONV_ROWS, D_MODEL), F32) + bias
        for j in range(CONV_WIDTH):
            acc = acc + ext_ref[r0 + first + j:r0 + first + j + CONV_ROWS, :] * w_ref[j:j + 1, :]
        c_ref[0, r0:r0 + CONV_ROWS, :] = acc
    ext_ref[0:HIST_PAD, :] = ext_ref[tt:tt + HIST_PAD, :]


def _conv_prompt(u, w_dw, b_dw, tt):
    bsz, t_len, _ = u.shape
    blk = pl.BlockSpec((1, tt, D_MODEL), lambda b, t: (b, t, 0))
    return pl.pallas_call(
        _conv_prompt_kernel,
        grid=(bsz, t_len // tt),
        in_specs=[
            blk,
            pl.BlockSpec((CONV_WIDTH, D_MODEL), lambda b, t: (0, 0)),
            pl.BlockSpec((1, D_MODEL), lambda b, t: (0, 0)),
        ],
        out_specs=blk,
        out_shape=jax.ShapeDtypeStruct(u.shape, F32),
        scratch_shapes=[pltpu.VMEM((HIST_PAD + tt, D_MODEL), F32)],
        compiler_params=_cparams("arbitrary", "arbitrary"),
        name="conv_prompt",
    )(u, w_dw, b_dw)


def _conv_sample_kernel(full_ref, w_ref, b_ref, c_ref):
    bb = full_ref.shape[0]
    t_new = c_ref.shape[1]
    w = w_ref[...]
    bias = b_ref[...]
    for b in range(bb):
        for i in range(t_new):
            win = full_ref[b, i:i + CONV_WIDTH, :]
            c_ref[b, i:i + 1, :] = jnp.sum(win * w, axis=0, keepdims=True) + bias


def _conv_sample(full, w_dw, b_dw, bb):
    bsz, rows, _ = full.shape
    t_new = rows - (CONV_WIDTH - 1)
    return pl.pallas_call(
        _conv_sample_kernel,
        grid=(bsz // bb,),
        in_specs=[
            pl.BlockSpec((bb, rows, D_MODEL), lambda i: (i, 0, 0)),
            pl.BlockSpec((CONV_WIDTH, D_MODEL), lambda i: (0, 0)),
            pl.BlockSpec((1, D_MODEL), lambda i: (0, 0)),
        ],
        out_specs=pl.BlockSpec((bb, t_new, D_MODEL), lambda i: (i, 0, 0)),
        out_shape=jax.ShapeDtypeStruct((bsz, t_new, D_MODEL), F32),
        compiler_params=_cparams("arbitrary"),
        name="conv_sample",
    )(full, w_dw, b_dw)


def _proj_kernel(a_ref, w_ref, b_ref, res_ref, lng_ref, lnb_ref, gf_ref, y_ref, hn_ref, *, pre_ln):
    a = a_ref[...]
    if pre_ln:
        a = a.astype(F32)
        mu = jnp.mean(a, axis=-1, keepdims=True)
        ac = a - mu
        var = jnp.mean(ac * ac, axis=-1, keepdims=True)
        a = ac * lax.rsqrt(var + LN_EPS) * lng_ref[...] + lnb_ref[...]
        a = a * jax.nn.sigmoid(a)
    y = res_ref[...] + jnp.dot(a.astype(BF16), w_ref[...], preferred_element_type=F32) + b_ref[...]
    y_ref[...] = y
    hn_ref[...] = _rms(y, gf_ref[...])


def _proj(a, w_bf, b, res, ln_g, ln_b, g_ffn, tm, pre_ln):
    n = a.shape[0]
    row = pl.BlockSpec((tm, D_MODEL), lambda i: (i, 0))
    vec = pl.BlockSpec((1, D_MODEL), lambda i: (0, 0))
    return pl.pallas_call(
        functools.partial(_proj_kernel, pre_ln=pre_ln),
        grid=(n // tm,),
        in_specs=[row, pl.BlockSpec((D_MODEL, D_MODEL), lambda i: (0, 0)), vec, row, vec, vec, vec],
        out_specs=[row, row],
        out_shape=[jax.ShapeDtypeStruct((n, D_MODEL), F32), jax.ShapeDtypeStruct((n, D_MODEL), F32)],
        compiler_params=_cparams("arbitrary"),
        name="proj_ln" if pre_ln else "proj",
    )(a, w_bf, b, res, ln_g, ln_b, g_ffn)


def _router_kernel(hn_ref, w_ref, b_ref, comb_ref):
    logits = jnp.dot(hn_ref[...], w_ref[...], preferred_element_type=F32,
                     precision=lax.Precision.HIGHEST) + b_ref[...]
    lane = lax.broadcasted_iota(jnp.int32, logits.shape, 1)
    neg = jnp.float32(-jnp.inf)
    cmask = (lane >= N_EXPERTS) & (lane < N_EXPERTS + N_GROUPS)
    cl = jnp.where(cmask, logits, neg)
    cmax = jnp.max(cl, axis=-1, keepdims=True)
    g_lane = jnp.min(jnp.where(cl == cmax, lane, LANES), axis=-1, keepdims=True)
    g_top = 1.0 / jnp.sum(jnp.exp(cl - cmax), axis=-1, keepdims=True)
    lo = (g_lane - N_EXPERTS) * EXPERTS_PER_GROUP
    fmask = (lane >= lo) & (lane < lo + EXPERTS_PER_GROUP)
    fl = jnp.where(fmask, logits, neg)
    m1 = jnp.max(fl, axis=-1, keepdims=True)
    i1 = jnp.min(jnp.where(fl == m1, lane, LANES), axis=-1, keepdims=True)
    fl2 = jnp.where(lane == i1, neg, fl)
    m2 = jnp.max(fl2, axis=-1, keepdims=True)
    i2 = jnp.min(jnp.where(fl2 == m2, lane, LANES), axis=-1, keepdims=True)
    r = jnp.exp(m2 - m1)
    w1 = 1.0 / (1.0 + r)
    w2 = r * w1
    comb_ref[...] = g_top * (jnp.where(lane == i1, w1, 0.0) + jnp.where(lane == i2, w2, 0.0))


def _router(hn, w_r, b_r, tm):
    n = hn.shape[0]
    return pl.pallas_call(
        _router_kernel,
        grid=(n // tm,),
        in_specs=[
            pl.BlockSpec((tm, D_MODEL), lambda i: (i, 0)),
            pl.BlockSpec((D_MODEL, LANES), lambda i: (0, 0)),
            pl.BlockSpec((1, LANES), lambda i: (0, 0)),
        ],
        out_specs=pl.BlockSpec((tm, LANES), lambda i: (i, 0)),
        out_shape=jax.ShapeDtypeStruct((n, LANES), F32),
        compiler_params=_cparams("arbitrary"),
        name="router",
    )(hn, w_r, b_r)


def _moe_kernel(hn_ref, comb_ref, wg_ref, wu_ref, wd_ref, res_ref, gfin_ref, y_ref, x_ref, acc_ref, *, final_norm):
    e = pl.program_id(1)

    @pl.when(e == 0)
    def _():
        x_ref[...] = hn_ref[...].astype(BF16)
        acc_ref[...] = res_ref[...]

    x = x_ref[...]
    comb = comb_ref[...]
    lane = lax.broadcasted_iota(jnp.int32, comb.shape, 1)
    gate = jnp.sum(jnp.where(lane == e, comb, 0.0), axis=-1, keepdims=True)
    hg = jnp.dot(x, wg_ref[0], preferred_element_type=F32)
    hu = jnp.dot(x, wu_ref[0], preferred_element_type=F32)
    h = hg * jax.nn.sigmoid(hg) * hu * gate
    acc_ref[...] += jnp.dot(h.astype(BF16), wd_ref[0], preferred_element_type=F32)

    @pl.when(e == pl.num_programs(1) - 1)
    def _():
        y = acc_ref[...]
        if final_norm:
            y = _rms(y, gfin_ref[...])
        y_ref[...] = y


def _moe(hn, comb, wg_bf, wu_bf, wd_bf, res, g_final, tm, final_norm):
    n = hn.shape[0]
    row = pl.BlockSpec((tm, D_MODEL), lambda i, e: (i, 0))
    return pl.pallas_call(
        functools.partial(_moe_kernel, final_norm=final_norm),
        grid=(n // tm, N_EXPERTS),
        in_specs=[
            row,
            pl.BlockSpec((tm, LANES), lambda i, e: (i, 0)),
            pl.BlockSpec((1, D_MODEL, D_EXPERT), lambda i, e: (e, 0, 0)),
            pl.BlockSpec((1, D_MODEL, D_EXPERT), lambda i, e: (e, 0, 0)),
            pl.BlockSpec((1, D_EXPERT, D_MODEL), lambda i, e: (e, 0, 0)),
            row,
            pl.BlockSpec((1, D_MODEL), lambda i, e: (0, 0)),
        ],
        out_specs=row,
        out_shape=jax.ShapeDtypeStruct((n, D_MODEL), F32),
        scratch_shapes=[pltpu.VMEM((tm, D_MODEL), BF16), pltpu.VMEM((tm, D_MODEL), F32)],
        compiler_params=_cparams("arbitrary", "arbitrary"),
        name="moe_final" if final_norm else "moe",
    )(hn, comb, wg_bf, wu_bf, wd_bf, res, g_final)


ATT_BLK = 128
ATT_TQ = 512


def _suffix_ones(n):
    r = lax.broadcasted_iota(jnp.int32, (n, n), 0)
    c = lax.broadcasted_iota(jnp.int32, (n, n), 1)
    return jnp.where(r >= c, 1.0, 0.0).astype(BF16)


def _sb_block(z, ones_ge, v_blk, carry, mask):
    lf = -_softplus(z)
    if mask is not None:
        lf = jnp.where(mask, lf, 0.0)
    hi = lf.astype(BF16)
    lo = (lf - hi.astype(F32)).astype(BF16)
    c = jnp.dot(hi, ones_ge, preferred_element_type=F32) + jnp.dot(lo, ones_ge, preferred_element_type=F32)
    a = jnp.exp(z + c + carry)
    if mask is not None:
        a = jnp.where(mask, a, 0.0)
    o = jnp.dot(a.astype(BF16), v_blk, preferred_element_type=F32)
    return o, carry + c[:, 0:1]


def _attn_prompt_kernel(bias_ref, q_ref, k_ref, v_ref, o_ref):
    hp = pl.program_id(1)
    qi = pl.program_id(2)
    tq = q_ref.shape[1]
    nsub = tq // ATT_BLK
    ones_ge = _suffix_ones(ATT_BLK)
    lane = lax.broadcasted_iota(jnp.int32, (ATT_BLK, LANES), 1)
    row = lax.broadcasted_iota(jnp.int32, (ATT_BLK, ATT_BLK), 0)
    col = lax.broadcasted_iota(jnp.int32, (ATT_BLK, ATT_BLK), 1)
    diag_mask = col < row
    for sq in range(nsub):
        q = q_ref[0, sq * ATT_BLK:(sq + 1) * ATT_BLK, :]
        d = qi * nsub + sq
        outs = []
        for h in range(2):
            head_lanes = (lane >= h * HEAD_DIM) & (lane < (h + 1) * HEAD_DIM)
            qh = jnp.where(head_lanes, q, jnp.zeros_like(q))
            bias = bias_ref[2 * hp + h]

            def scores(j, qh=qh, bias=bias):
                k0 = pl.multiple_of(j * ATT_BLK, ATT_BLK)
                kb = k_ref[0, pl.ds(k0, ATT_BLK), :]
                vb = v_ref[0, pl.ds(k0, ATT_BLK), :]
                z = lax.dot_general(qh, kb, (((1,), (1,)), ((), ())), preferred_element_type=F32) + bias
                return z, vb

            z, vb = scores(d)
            acc, carry = _sb_block(z, ones_ge, vb, jnp.zeros((ATT_BLK, 1), F32), diag_mask)

            def body(it, st, scores=scores, d=d):
                acc, carry = st
                z, vb = scores(d - 1 - it)
                o, carry = _sb_block(z, ones_ge, vb, carry, None)
                return acc + o, carry

            acc, carry = lax.fori_loop(0, d, body, (acc, carry))
            outs.append(acc)
        o_ref[0, sq * ATT_BLK:(sq + 1) * ATT_BLK, :] = jnp.where(lane < HEAD_DIM, outs[0], outs[1]).astype(o_ref.dtype)


def _attn_prompt(q, kb, vb, bias, bsz, t_len):
    q3 = q.reshape(bsz, t_len, D_MODEL)
    k3 = kb.reshape(bsz, t_len, D_MODEL)
    v3 = vb.reshape(bsz, t_len, D_MODEL)
    qblk = pl.BlockSpec((1, ATT_TQ, LANES), lambda b, hp, qi: (b, qi, hp))
    kvblk = pl.BlockSpec((1, t_len, LANES), lambda b, hp, qi: (b, 0, hp))
    o = pl.pallas_call(
        _attn_prompt_kernel,
        grid=(bsz, D_MODEL // LANES, t_len // ATT_TQ),
        in_specs=[pl.BlockSpec(memory_space=pltpu.SMEM), qblk, kvblk, kvblk],
        out_specs=qblk,
        out_shape=jax.ShapeDtypeStruct((bsz, t_len, D_MODEL), BF16),
        compiler_params=_cparams("arbitrary", "arbitrary", "arbitrary"),
        name="attn_prompt",
    )(bias, q3, k3, v3)
    return o.reshape(bsz * t_len, D_MODEL)


SAMPLE_PAGES_PER_STEP = 8


def _attn_sample_kernel(pt_ref, q_ref, kn_ref, vn_ref, bias_ref, *refs):
    npg = SAMPLE_PAGES_PER_STEP
    k_refs = refs[:npg]
    v_refs = refs[npg:2 * npg]
    o_ref = refs[2 * npg]
    qbd_ref, acc_ref, carry_ref = refs[2 * npg + 1:]
    j = pl.program_id(1)
    t_new = q_ref.shape[1]
    rows = t_new * N_HEADS
    lane = lax.broadcasted_iota(jnp.int32, (N_HEADS, D_MODEL), 1)
    sub = lax.broadcasted_iota(jnp.int32, (N_HEADS, D_MODEL), 0)
    head_lanes = (lane >= sub * HEAD_DIM) & (lane < (sub + 1) * HEAD_DIM)
    bias = bias_ref[...]

    @pl.when(j == 0)
    def _():
        q = q_ref[0].astype(F32)
        qbd = jnp.concatenate(
            [jnp.where(head_lanes, jnp.broadcast_to(q[i:i + 1, :], (N_HEADS, D_MODEL)), 0.0) for i in range(t_new)],
            axis=0)
        qbd_ref[...] = qbd.astype(BF16)
        qi = lax.broadcasted_iota(jnp.int32, (rows, 1), 0) // N_HEADS
        kn = kn_ref[0]
        vn = vn_ref[0]
        carry = jnp.zeros((rows, 1), F32)
        acc = jnp.zeros((rows, D_MODEL), F32)
        for m in range(t_new - 1, -1, -1):
            valid = qi > m
            z = jnp.sum(qbd * kn[m:m + 1, :], axis=-1, keepdims=True) + bias
            lf = jnp.where(valid, -_softplus(z), 0.0)
            a = jnp.where(valid, jnp.exp(z + lf + carry), 0.0)
            acc = acc + a * vn[m:m + 1, :]
            carry = carry + lf
        acc_ref[...] = acc
        carry_ref[...] = carry

    ones_ge = _suffix_ones(PAGE_SIZE)
    qbd = qbd_ref[...]
    acc = acc_ref[...]
    carry = carry_ref[...]
    for r in range(npg):
        kp = k_refs[r][0].astype(BF16)
        vp = v_refs[r][0].astype(BF16)
        z = lax.dot_general(qbd, kp, (((1,), (1,)), ((), ())), preferred_element_type=F32) + bias
        o, carry = _sb_block(z, ones_ge, vp, carry, None)
        acc = acc + o
    acc_ref[...] = acc
    carry_ref[...] = carry

    @pl.when(j == pl.num_programs(1) - 1)
    def _():
        for i in range(t_new):
            blk = jnp.where(head_lanes, acc[i * N_HEADS:(i + 1) * N_HEADS, :], 0.0)
            o_ref[0, i:i + 1, :] = jnp.sum(blk, axis=0, keepdims=True).astype(o_ref.dtype)


def _attn_sample(q, k_new, v_new, cache_k, cache_v, page_table, bias):
    bsz, t_new, _ = q.shape
    n_pages = page_table.shape[1]
    npg = SAMPLE_PAGES_PER_STEP
    steps = n_pages // npg
    rows = t_new * N_HEADS
    bias_col = jnp.tile(bias.astype(F32), t_new).reshape(rows, 1)
    tok = pl.BlockSpec((1, t_new, D_MODEL), lambda b, j, pt: (b, 0, 0))

    def page_spec(r):
        return pl.BlockSpec((1, PAGE_SIZE, D_MODEL), lambda b, j, pt: (pt[b, n_pages - 1 - j * npg - r], 0, 0))

    grid_spec = pltpu.PrefetchScalarGridSpec(
        num_scalar_prefetch=1,
        grid=(bsz, steps),
        in_specs=[tok, tok, tok, pl.BlockSpec((rows, 1), lambda b, j, pt: (0, 0))]
        + [page_spec(r) for r in range(npg)] + [page_spec(r) for r in range(npg)],
        out_specs=tok,
        scratch_shapes=[
            pltpu.VMEM((rows, D_MODEL), BF16),
            pltpu.VMEM((rows, D_MODEL), F32),
            pltpu.VMEM((rows, 1), F32),
        ],
    )
    return pl.pallas_call(
        _attn_sample_kernel,
        grid_spec=grid_spec,
        out_shape=jax.ShapeDtypeStruct((bsz, t_new, D_MODEL), BF16),
        compiler_params=_cparams("arbitrary", "arbitrary"),
        name="attn_sample",
    )(page_table, q, k_new, v_new, bias_col, *([cache_k] * npg), *([cache_v] * npg))


def _router_params(w_coarse, b_coarse, w_fine, b_fine):
    wf = jnp.transpose(w_fine, (1, 0, 2)).reshape(D_MODEL, N_EXPERTS)
    w = jnp.concatenate([wf, w_coarse, jnp.zeros((D_MODEL, LANES - N_EXPERTS - N_GROUPS), F32)], axis=1)
    b = jnp.concatenate([b_fine.reshape(-1), b_coarse, jnp.zeros((LANES - N_EXPERTS - N_GROUPS,), F32)])
    return w, b.reshape(1, LANES)


def _ffn(hn, res, mw, g_final, tm, final_norm):
    w_r, b_r, wg, wu, wd = mw
    comb = _router(hn, w_r, b_r, tm)
    return _moe(hn, comb, wg, wu, wd, res, g_final, tm, final_norm)


def kernel(x_prompt, x_sample, state_conv, cache_k, cache_v, page_table, norm_mix, norm_ffn, norm_final,
           conv_w_pw1, conv_b_pw1, conv_w_dw, conv_b_dw, conv_ln_g, conv_ln_b, conv_w_pw2, conv_b_pw2,
           sb_w_qkv, sb_w_o, sb_logit_bias, moe_w_coarse, moe_b_coarse, moe_w_fine, moe_b_fine,
           moe_w_gate, moe_w_up, moe_w_down):
    bsz, t_len, _ = x_prompt.shape
    dbsz, t_new, _ = x_sample.shape
    n_p = bsz * t_len
    n_s = dbsz * t_new
    tm_p, tm_s = 512, n_s
    vec = lambda a: a.reshape(1, -1)
    g_fin = vec(norm_final)

    xp = x_prompt.reshape(n_p, D_MODEL)
    xs = x_sample.reshape(n_s, D_MODEL)

    moe_w = []
    for i in range(2):
        w_r, b_r = _router_params(moe_w_coarse[i], moe_b_coarse[i], moe_w_fine[i], moe_b_fine[i])
        moe_w.append((w_r, b_r, moe_w_gate[i].astype(BF16), moe_w_up[i].astype(BF16), moe_w_down[i].astype(BF16)))

    w1 = conv_w_pw1[0].astype(BF16)
    w2 = conv_w_pw2[0].astype(BF16)
    up = _pw1_glu(xp, vec(norm_mix[0]), w1, vec(conv_b_pw1[0]), tm_p)
    us = _pw1_glu(xs, vec(norm_mix[0]), w1, vec(conv_b_pw1[0]), tm_s)
    up3 = up.reshape(bsz, t_len, D_MODEL)
    full_s = jnp.concatenate([state_conv[0], us.reshape(dbsz, t_new, D_MODEL)], axis=1)
    conv_state_p = up3[:, t_len - (CONV_WIDTH - 1):, :][None]
    conv_state_s = full_s[:, t_new:, :][None]
    cp = _conv_prompt(up3, conv_w_dw[0], vec(conv_b_dw[0]), 256).reshape(n_p, D_MODEL)
    cs = _conv_sample(full_s, conv_w_dw[0], vec(conv_b_dw[0]), 8).reshape(n_s, D_MODEL)
    ln = (vec(conv_ln_g[0]), vec(conv_ln_b[0]), vec(norm_ffn[0]))
    yp, hp = _proj(cp, w2, vec(conv_b_pw2[0]), xp, *ln, tm_p, True)
    ys, hs = _proj(cs, w2, vec(conv_b_pw2[0]), xs, *ln, tm_s, True)
    yp = _ffn(hp, yp, moe_w[0], g_fin, tm_p, False)
    ys = _ffn(hs, ys, moe_w[0], g_fin, tm_s, False)

    wqkv = sb_w_qkv[0].astype(BF16)
    wo = sb_w_o[0].astype(BF16)
    bias = sb_logit_bias[0].astype(F32)
    qp, kp, vp, kbp, vbp = _qkv(yp, vec(norm_mix[1]), wqkv, tm_p)
    qs, ks, vs, _, _ = _qkv(ys, vec(norm_mix[1]), wqkv, tm_s)
    op = _attn_prompt(qp, kbp, vbp, bias, bsz, t_len)
    n_pool = cache_k.shape[1]
    os_ = _attn_sample(qs.reshape(dbsz, t_new, D_MODEL), ks.reshape(dbsz, t_new, D_MODEL),
                       vs.reshape(dbsz, t_new, D_MODEL),
                       cache_k[0].reshape(n_pool, PAGE_SIZE, D_MODEL), cache_v[0].reshape(n_pool, PAGE_SIZE, D_MODEL),
                       page_table, bias).reshape(n_s, D_MODEL)
    zero_b = jnp.zeros((1, D_MODEL), F32)
    nf1 = vec(norm_ffn[1])
    yp, hp = _proj(op, wo, zero_b, yp, nf1, nf1, nf1, tm_p, False)
    ys, hs = _proj(os_, wo, zero_b, ys, nf1, nf1, nf1, tm_s, False)
    yp = _ffn(hp, yp, moe_w[1], g_fin, tm_p, True)
    ys = _ffn(hs, ys, moe_w[1], g_fin, tm_s, True)

    hshape_p = (1, bsz, t_len, N_HEADS, HEAD_DIM)
    hshape_s = (1, dbsz, t_new, N_HEADS, HEAD_DIM)
    return (yp.reshape(bsz, t_len, D_MODEL), ys.reshape(dbsz, t_new, D_MODEL), conv_state_p, conv_state_s,
            kp.reshape(hshape_p), vp.reshape(hshape_p), ks.reshape(hshape_s), vs.reshape(hshape_s))
```

```python
import functools

import jax
import jax.numpy as jnp
from jax import lax
from jax.experimental import pallas as pl
from jax.experimental.pallas import tpu as pltpu

D_MODEL = 1024
CONV_WIDTH = 31
N_HEADS = 16
HEAD_DIM = D_MODEL // N_HEADS
N_GROUPS = 4
EXPERTS_PER_GROUP = 8
N_EXPERTS = N_GROUPS * EXPERTS_PER_GROUP
D_EXPERT = D_MODEL // 4
PAGE_SIZE = 128
NORM_EPS = 1e-6
LN_EPS = 1e-5

LANES = 128
SUBLANES = 8
HIST_PAD = 32
VMEM_LIMIT = 56 * 1024 * 1024

F32 = jnp.float32
BF16 = jnp.bfloat16


def _cparams(*sem):
    return pltpu.CompilerParams(dimension_semantics=sem, vmem_limit_bytes=VMEM_LIMIT)


def _rms(x, g):
    return x * lax.rsqrt(jnp.mean(x * x, axis=-1, keepdims=True) + NORM_EPS) * g


LOG2E = 1.4426950408889634
POW2_CLAMP = 64.0


def _softplus2(z2):
    return jnp.where(z2 > POW2_CLAMP, z2, jnp.log2(1.0 + jnp.exp2(jnp.minimum(z2, POW2_CLAMP))))


def _pw1_glu_kernel(x_ref, g_ref, w_ref, b_ref, u_ref):
    h = _rms(x_ref[...], g_ref[...]).astype(BF16)
    a = jnp.dot(h, w_ref[...], preferred_element_type=F32) + b_ref[...]
    u_ref[...] = a[:, :D_MODEL] * jax.nn.sigmoid(a[:, D_MODEL:])


def _pw1_glu(x, g, w_bf, b, tm):
    n = x.shape[0]
    return pl.pallas_call(
        _pw1_glu_kernel,
        grid=(n // tm,),
        in_specs=[
            pl.BlockSpec((tm, D_MODEL), lambda i: (i, 0)),
            pl.BlockSpec((1, D_MODEL), lambda i: (0, 0)),
            pl.BlockSpec((D_MODEL, 2 * D_MODEL), lambda i: (0, 0)),
            pl.BlockSpec((1, 2 * D_MODEL), lambda i: (0, 0)),
        ],
        out_specs=pl.BlockSpec((tm, D_MODEL), lambda i: (i, 0)),
        out_shape=jax.ShapeDtypeStruct((n, D_MODEL), F32),
        compiler_params=_cparams("arbitrary"),
        name="pw1_glu",
    )(x, g, w_bf, b)


Q_SCALE2 = (HEAD_DIM ** -0.5) * LOG2E


def _qkv_kernel(x_ref, g_ref, w_ref, q_ref, k_ref, v_ref):
    h = _rms(x_ref[...], g_ref[...]).astype(BF16)
    a = jnp.dot(h, w_ref[...], preferred_element_type=F32)
    q_ref[...] = (a[:, :D_MODEL] * Q_SCALE2).astype(BF16)
    k_ref[...] = a[:, D_MODEL:2 * D_MODEL]
    v_ref[...] = a[:, 2 * D_MODEL:]


def _qkv(x, g, w_bf, tm):
    n = x.shape[0]
    row = pl.BlockSpec((tm, D_MODEL), lambda i: (i, 0))
    return pl.pallas_call(
        _qkv_kernel,
        grid=(n // tm,),
        in_specs=[
            row,
            pl.BlockSpec((1, D_MODEL), lambda i: (0, 0)),
            pl.BlockSpec((D_MODEL, 3 * D_MODEL), lambda i: (0, 0)),
        ],
        out_specs=[row, row, row],
        out_shape=[
            jax.ShapeDtypeStruct((n, D_MODEL), BF16),
            jax.ShapeDtypeStruct((n, D_MODEL), F32),
            jax.ShapeDtypeStruct((n, D_MODEL), F32),
        ],
        compiler_params=_cparams("arbitrary"),
        name="qkv",
    )(x, g, w_bf)


def _qkv_t_kernel(x_ref, g_ref, wq_ref, wkt_ref, wvt_ref, q_ref, kt_ref, vt_ref, ktb_ref, vtb_ref):
    h = _rms(x_ref[0], g_ref[...]).astype(BF16)
    q_ref[0] = (jnp.dot(h, wq_ref[...], preferred_element_type=F32) * Q_SCALE2).astype(BF16)
    nt = (((1,), (1,)), ((), ()))
    kt = lax.dot_general(wkt_ref[...], h, nt, preferred_element_type=F32)
    vt = lax.dot_general(wvt_ref[...], h, nt, preferred_element_type=F32)
    kt_ref[0] = kt
    vt_ref[0] = vt
    ktb_ref[0] = kt.astype(BF16)
    vtb_ref[0] = vt.astype(BF16)


def _qkv_t(x3, g, wq_bf, wkt_bf, wvt_bf, tm):
    bsz, t_len, _ = x3.shape
    row = pl.BlockSpec((1, tm, D_MODEL), lambda b, t: (b, t, 0))
    col = pl.BlockSpec((1, D_MODEL, tm), lambda b, t: (b, 0, t))
    wspec = pl.BlockSpec((D_MODEL, D_MODEL), lambda b, t: (0, 0))
    tshape = (bsz, D_MODEL, t_len)
    return pl.pallas_call(
        _qkv_t_kernel,
        grid=(bsz, t_len // tm),
        in_specs=[row, pl.BlockSpec((1, D_MODEL), lambda b, t: (0, 0)), wspec, wspec, wspec],
        out_specs=[row, col, col, col, col],
        out_shape=[
            jax.ShapeDtypeStruct((bsz, t_len, D_MODEL), BF16),
            jax.ShapeDtypeStruct(tshape, F32),
            jax.ShapeDtypeStruct(tshape, F32),
            jax.ShapeDtypeStruct(tshape, BF16),
            jax.ShapeDtypeStruct(tshape, BF16),
        ],
        compiler_params=_cparams("arbitrary", "arbitrary"),
        name="qkv_t",
    )(x3, g, wq_bf, wkt_bf, wvt_bf)


CONV_ROWS = 16


def _conv_prompt_kernel(u_ref, w_ref, b_ref, c_ref, sh_ref):
    tt = u_ref.shape[1]
    rows = HIST_PAD + tt
    t = pl.program_id(1)

    @pl.when(t == 0)
    def _():
        sh_ref[0, 0:HIST_PAD, :] = jnp.zeros((HIST_PAD, D_MODEL), F32)

    sh_ref[0, HIST_PAD:rows, :] = u_ref[0]
    for s in range(1, SUBLANES):
        sh_ref[s] = pltpu.roll(sh_ref[0], rows - s, axis=0)
    first = HIST_PAD - (CONV_WIDTH - 1)
    bias = b_ref[...]

    def body(r, carry):
        r0 = pl.multiple_of(r * CONV_ROWS, CONV_ROWS)
        accs = [jnp.zeros((SUBLANES, D_MODEL), F32) + bias for _ in range(CONV_ROWS // SUBLANES)]
        for j in range(CONV_WIDTH):
            s, a = (first + j) % SUBLANES, (first + j) // SUBLANES
            w = w_ref[j * SUBLANES:(j + 1) * SUBLANES, :]
            for i in range(len(accs)):
                accs[i] = accs[i] + sh_ref[s, pl.ds(r0 + (a + i) * SUBLANES, SUBLANES), :] * w
        for i, acc in enumerate(accs):
            c_ref[0, pl.ds(r0 + i * SUBLANES, SUBLANES), :] = acc
        return carry

    lax.fori_loop(0, tt // CONV_ROWS, body, 0)
    sh_ref[0, 0:HIST_PAD, :] = sh_ref[0, tt:rows, :]


def _conv_prompt(u, w_dw, b_dw, tt):
    bsz, t_len, _ = u.shape
    blk = pl.BlockSpec((1, tt, D_MODEL), lambda b, t: (b, t, 0))
    w_rep = jnp.repeat(w_dw, SUBLANES, axis=0)
    return pl.pallas_call(
        _conv_prompt_kernel,
        grid=(bsz, t_len // tt),
        in_specs=[
            blk,
            pl.BlockSpec((CONV_WIDTH * SUBLANES, D_MODEL), lambda b, t: (0, 0)),
            pl.BlockSpec((1, D_MODEL), lambda b, t: (0, 0)),
        ],
        out_specs=blk,
        out_shape=jax.ShapeDtypeStruct(u.shape, F32),
        scratch_shapes=[pltpu.VMEM((SUBLANES, HIST_PAD + tt, D_MODEL), F32)],
        compiler_params=_cparams("arbitrary", "arbitrary"),
        name="conv_prompt",
    )(u, w_rep, b_dw)


def _conv_sample_kernel(full_ref, w_ref, b_ref, c_ref):
    bb = full_ref.shape[0]
    t_new = c_ref.shape[1]
    w = w_ref[...]
    bias = b_ref[...]
    for b in range(bb):
        for i in range(t_new):
            win = full_ref[b, i:i + CONV_WIDTH, :]
            c_ref[b, i:i + 1, :] = jnp.sum(win * w, axis=0, keepdims=True) + bias


def _conv_sample(full, w_dw, b_dw, bb):
    bsz, rows, _ = full.shape
    t_new = rows - (CONV_WIDTH - 1)
    return pl.pallas_call(
        _conv_sample_kernel,
        grid=(bsz // bb,),
        in_specs=[
            pl.BlockSpec((bb, rows, D_MODEL), lambda i: (i, 0, 0)),
            pl.BlockSpec((CONV_WIDTH, D_MODEL), lambda i: (0, 0)),
            pl.BlockSpec((1, D_MODEL), lambda i: (0, 0)),
        ],
        out_specs=pl.BlockSpec((bb, t_new, D_MODEL), lambda i: (i, 0, 0)),
        out_shape=jax.ShapeDtypeStruct((bsz, t_new, D_MODEL), F32),
        compiler_params=_cparams("arbitrary"),
        name="conv_sample",
    )(full, w_dw, b_dw)


def _proj_kernel(a_ref, w_ref, b_ref, res_ref, lng_ref, lnb_ref, gf_ref, y_ref, hn_ref, *, pre_ln):
    a = a_ref[...]
    if pre_ln:
        a = a.astype(F32)
        mu = jnp.mean(a, axis=-1, keepdims=True)
        ac = a - mu
        var = jnp.mean(ac * ac, axis=-1, keepdims=True)
        a = ac * lax.rsqrt(var + LN_EPS) * lng_ref[...] + lnb_ref[...]
        a = a * jax.nn.sigmoid(a)
    y = res_ref[...] + jnp.dot(a.astype(BF16), w_ref[...], preferred_element_type=F32) + b_ref[...]
    y_ref[...] = y
    hn_ref[...] = _rms(y, gf_ref[...])


def _proj(a, w_bf, b, res, ln_g, ln_b, g_ffn, tm, pre_ln):
    n = a.shape[0]
    row = pl.BlockSpec((tm, D_MODEL), lambda i: (i, 0))
    vec = pl.BlockSpec((1, D_MODEL), lambda i: (0, 0))
    return pl.pallas_call(
        functools.partial(_proj_kernel, pre_ln=pre_ln),
        grid=(n // tm,),
        in_specs=[row, pl.BlockSpec((D_MODEL, D_MODEL), lambda i: (0, 0)), vec, row, vec, vec, vec],
        out_specs=[row, row],
        out_shape=[jax.ShapeDtypeStruct((n, D_MODEL), F32), jax.ShapeDtypeStruct((n, D_MODEL), F32)],
        compiler_params=_cparams("arbitrary"),
        name="proj_ln" if pre_ln else "proj",
    )(a, w_bf, b, res, ln_g, ln_b, g_ffn)


def _router_kernel(hn_ref, w_ref, b_ref, comb_ref):
    logits = jnp.dot(hn_ref[...], w_ref[...], preferred_element_type=F32,
                     precision=lax.Precision.HIGHEST) + b_ref[...]
    lane = lax.broadcasted_iota(jnp.int32, logits.shape, 1)
    neg = jnp.float32(-jnp.inf)
    cmask = (lane >= N_EXPERTS) & (lane < N_EXPERTS + N_GROUPS)
    cl = jnp.where(cmask, logits, neg)
    cmax = jnp.max(cl, axis=-1, keepdims=True)
    g_lane = jnp.min(jnp.where(cl == cmax, lane, LANES), axis=-1, keepdims=True)
    g_top = 1.0 / jnp.sum(jnp.exp(cl - cmax), axis=-1, keepdims=True)
    lo = (g_lane - N_EXPERTS) * EXPERTS_PER_GROUP
    fmask = (lane >= lo) & (lane < lo + EXPERTS_PER_GROUP)
    fl = jnp.where(fmask, logits, neg)
    m1 = jnp.max(fl, axis=-1, keepdims=True)
    i1 = jnp.min(jnp.where(fl == m1, lane, LANES), axis=-1, keepdims=True)
    fl2 = jnp.where(lane == i1, neg, fl)
    m2 = jnp.max(fl2, axis=-1, keepdims=True)
    i2 = jnp.min(jnp.where(fl2 == m2, lane, LANES), axis=-1, keepdims=True)
    r = jnp.exp(m2 - m1)
    w1 = 1.0 / (1.0 + r)
    w2 = r * w1
    comb_ref[...] = g_top * (jnp.where(lane == i1, w1, 0.0) + jnp.where(lane == i2, w2, 0.0))


def _router(hn, w_r, b_r, tm):
    n = hn.shape[0]
    return pl.pallas_call(
        _router_kernel,
        grid=(n // tm,),
        in_specs=[
            pl.BlockSpec((tm, D_MODEL), lambda i: (i, 0)),
            pl.BlockSpec((D_MODEL, LANES), lambda i: (0, 0)),
            pl.BlockSpec((1, LANES), lambda i: (0, 0)),
        ],
        out_specs=pl.BlockSpec((tm, LANES), lambda i: (i, 0)),
        out_shape=jax.ShapeDtypeStruct((n, LANES), F32),
        compiler_params=_cparams("arbitrary"),
        name="router",
    )(hn, w_r, b_r)


def _moe_kernel(hn_ref, comb_ref, wg_ref, wu_ref, wd_ref, res_ref, gfin_ref, y_ref, x_ref, acc_ref, *, final_norm):
    e = pl.program_id(1)

    @pl.when(e == 0)
    def _():
        x_ref[...] = hn_ref[...].astype(BF16)
        acc_ref[...] = res_ref[...]

    x = x_ref[...]
    comb = comb_ref[...]
    lane = lax.broadcasted_iota(jnp.int32, comb.shape, 1)
    gate = jnp.sum(jnp.where(lane == e, comb, 0.0), axis=-1, keepdims=True)
    hg = jnp.dot(x, wg_ref[0], preferred_element_type=F32)
    hu = jnp.dot(x, wu_ref[0], preferred_element_type=F32)
    h = hg * jax.nn.sigmoid(hg) * hu * gate
    acc_ref[...] += jnp.dot(h.astype(BF16), wd_ref[0], preferred_element_type=F32)

    @pl.when(e == pl.num_programs(1) - 1)
    def _():
        y = acc_ref[...]
        if final_norm:
            y = _rms(y, gfin_ref[...])
        y_ref[...] = y


def _moe(hn, comb, wg_bf, wu_bf, wd_bf, res, g_final, tm, final_norm):
    n = hn.shape[0]
    row = pl.BlockSpec((tm, D_MODEL), lambda i, e: (i, 0))
    return pl.pallas_call(
        functools.partial(_moe_kernel, final_norm=final_norm),
        grid=(n // tm, N_EXPERTS),
        in_specs=[
            row,
            pl.BlockSpec((tm, LANES), lambda i, e: (i, 0)),
            pl.BlockSpec((1, D_MODEL, D_EXPERT), lambda i, e: (e, 0, 0)),
            pl.BlockSpec((1, D_MODEL, D_EXPERT), lambda i, e: (e, 0, 0)),
            pl.BlockSpec((1, D_EXPERT, D_MODEL), lambda i, e: (e, 0, 0)),
            row,
            pl.BlockSpec((1, D_MODEL), lambda i, e: (0, 0)),
        ],
        out_specs=row,
        out_shape=jax.ShapeDtypeStruct((n, D_MODEL), F32),
        scratch_shapes=[pltpu.VMEM((tm, D_MODEL), BF16), pltpu.VMEM((tm, D_MODEL), F32)],
        compiler_params=_cparams("arbitrary", "arbitrary"),
        name="moe_final" if final_norm else "moe",
    )(hn, comb, wg_bf, wu_bf, wd_bf, res, g_final)


ATT_BLK = 128
ATT_TQ = 512


NT_DIMS = (((1,), (1,)), ((), ()))


def _neg_suffix_ones(n):
    r = lax.broadcasted_iota(jnp.int32, (n, n), 0)
    c = lax.broadcasted_iota(jnp.int32, (n, n), 1)
    return jnp.where(r >= c, -1.0, 0.0).astype(BF16)


def _sb_weights(z2, neg_ge, run, mask):
    sp = _softplus2(z2)
    if mask is not None:
        sp = jnp.where(mask, sp, 0.0)
    c = jnp.dot(sp.astype(BF16), neg_ge, preferred_element_type=F32)
    a = jnp.exp2(z2 + c + run)
    if mask is not None:
        a = jnp.where(mask, a, 0.0)
    return a, run + c[:, 0:1]


def _attn_prompt_kernel(bias2_ref, q_ref, kt_ref, vt_ref, o_ref):
    hp = pl.program_id(1)
    qi = pl.program_id(2)
    nsub = ATT_TQ // ATT_BLK
    neg_ge = _neg_suffix_ones(ATT_BLK)
    lane = lax.broadcasted_iota(jnp.int32, (ATT_BLK, LANES), 1)
    row = lax.broadcasted_iota(jnp.int32, (ATT_BLK, ATT_BLK), 0)
    col = lax.broadcasted_iota(jnp.int32, (ATT_BLK, ATT_BLK), 1)
    diag_mask = col < row
    bias2 = [bias2_ref[2 * hp + h] for h in range(2)]

    chains = []
    for sq in range(nsub):
        q = q_ref[0, sq * ATT_BLK:(sq + 1) * ATT_BLK, :]
        for h in range(2):
            head = (lane >= h * HEAD_DIM) & (lane < (h + 1) * HEAD_DIM)
            chains.append((jnp.where(head, q, jnp.zeros_like(q)), bias2[h]))
    nch = len(chains)

    def chunk_pass(c0, nblks, runs, diagonal):
        kt_c = kt_ref[0, :, pl.ds(c0, ATT_TQ)]
        vt_c = vt_ref[0, :, pl.ds(c0, ATT_TQ)]
        blk = lambda x, b: x[:, b * ATT_BLK:(b + 1) * ATT_BLK]
        zs = [jnp.dot(qh, kt_c[:, :nb * ATT_BLK], preferred_element_type=F32) + bias
              for (qh, bias), nb in zip(chains, nblks)]
        runs = list(runs)
        parts = [[None] * nb for nb in nblks]
        for b in range(max(nblks) - 1, -1, -1):
            live = [n for n in range(nch) if nblks[n] > b]
            masked = {n: diagonal and b == nblks[n] - 1 for n in live}
            sps = {}
            for n in live:
                sp = _softplus2(blk(zs[n], b))
                if masked[n]:
                    sp = jnp.where(diag_mask, sp, 0.0)
                sps[n] = sp.astype(BF16)
            cs = {n: jnp.dot(sps[n], neg_ge, preferred_element_type=F32) for n in live}
            for n in live:
                a = jnp.exp2(blk(zs[n], b) + cs[n] + runs[n])
                if masked[n]:
                    a = jnp.where(diag_mask, a, 0.0)
                parts[n][b] = a.astype(BF16)
                runs[n] = runs[n] + cs[n][:, 0:1]
        outs = []
        for n in range(nch):
            a_c = parts[n][0] if nblks[n] == 1 else jnp.concatenate(parts[n], axis=1)
            outs.append(lax.dot_general(a_c, vt_c[:, :nblks[n] * ATT_BLK], NT_DIMS, preferred_element_type=F32))
        return outs, runs

    outs, runs = chunk_pass(pl.multiple_of(qi * ATT_TQ, ATT_TQ), [n // 2 + 1 for n in range(nch)],
                            [jnp.zeros((ATT_BLK, 1), F32)] * nch, True)
    state = [x for n in range(nch) for x in (outs[n], runs[n])]

    def body(it, st):
        outs, runs = chunk_pass(pl.multiple_of((qi - 1 - it) * ATT_TQ, ATT_TQ), [nsub] * nch,
                                [st[2 * n + 1] for n in range(nch)], False)
        return tuple(x for n in range(nch) for x in (st[2 * n] + outs[n], runs[n]))

    st = lax.fori_loop(0, qi, body, tuple(state))
    for sq in range(nsub):
        o_ref[0, sq * ATT_BLK:(sq + 1) * ATT_BLK, :] = jnp.where(
            lane < HEAD_DIM, st[4 * sq], st[4 * sq + 2]).astype(o_ref.dtype)


def _attn_prompt(q3, ktb, vtb, bias2):
    bsz, t_len, _ = q3.shape
    qblk = pl.BlockSpec((1, ATT_TQ, LANES), lambda b, hp, qi: (b, qi, hp))
    kvblk = pl.BlockSpec((1, LANES, t_len), lambda b, hp, qi: (b, hp, 0))
    o = pl.pallas_call(
        _attn_prompt_kernel,
        grid=(bsz, D_MODEL // LANES, t_len // ATT_TQ),
        in_specs=[pl.BlockSpec(memory_space=pltpu.SMEM), qblk, kvblk, kvblk],
        out_specs=qblk,
        out_shape=jax.ShapeDtypeStruct((bsz, t_len, D_MODEL), BF16),
        compiler_params=_cparams("arbitrary", "arbitrary", "arbitrary"),
        name="attn_prompt",
    )(bias2, q3, ktb, vtb)
    return o.reshape(bsz * t_len, D_MODEL)


SAMPLE_PAGES_PER_STEP = 8


def _attn_sample_kernel(pt_ref, q_ref, kn_ref, vn_ref, bias_ref, *refs):
    npg = SAMPLE_PAGES_PER_STEP
    k_refs = refs[:npg]
    v_refs = refs[npg:2 * npg]
    o_ref = refs[2 * npg]
    qbd_ref, acc_ref, carry_ref = refs[2 * npg + 1:]
    j = pl.program_id(1)
    t_new = q_ref.shape[1]
    rows = t_new * N_HEADS
    lane = lax.broadcasted_iota(jnp.int32, (N_HEADS, D_MODEL), 1)
    sub = lax.broadcasted_iota(jnp.int32, (N_HEADS, D_MODEL), 0)
    head_lanes = (lane >= sub * HEAD_DIM) & (lane < (sub + 1) * HEAD_DIM)
    bias = bias_ref[...]

    @pl.when(j == 0)
    def _():
        q = q_ref[0].astype(F32)
        qbd = jnp.concatenate(
            [jnp.where(head_lanes, jnp.broadcast_to(q[i:i + 1, :], (N_HEADS, D_MODEL)), 0.0) for i in range(t_new)],
            axis=0)
        qbd_ref[...] = qbd.astype(BF16)
        qi = lax.broadcasted_iota(jnp.int32, (rows, 1), 0) // N_HEADS
        kn = kn_ref[0]
        vn = vn_ref[0]
        carry = jnp.zeros((rows, 1), F32)
        acc = jnp.zeros((rows, D_MODEL), F32)
        for m in range(t_new - 1, -1, -1):
            valid = qi > m
            z2 = jnp.sum(qbd * kn[m:m + 1, :], axis=-1, keepdims=True) + bias
            sp = jnp.where(valid, _softplus2(z2), 0.0)
            a = jnp.where(valid, jnp.exp2(z2 - sp + carry), 0.0)
            acc = acc + a * vn[m:m + 1, :]
            carry = carry - sp
        acc_ref[...] = acc
        carry_ref[...] = carry

    neg_ge = _neg_suffix_ones(PAGE_SIZE)
    qbd = qbd_ref[...]
    acc = acc_ref[...]
    carry = carry_ref[...]
    for r in range(npg):
        ktp = k_refs[r][0].astype(BF16)
        vtp = v_refs[r][0].astype(BF16)
        z2 = jnp.dot(qbd, ktp, preferred_element_type=F32) + bias
        a, carry = _sb_weights(z2, neg_ge, carry, None)
        acc = acc + lax.dot_general(a.astype(BF16), vtp, NT_DIMS, preferred_element_type=F32)
    acc_ref[...] = acc
    carry_ref[...] = carry

    @pl.when(j == pl.num_programs(1) - 1)
    def _():
        for i in range(t_new):
            blk = jnp.where(head_lanes, acc[i * N_HEADS:(i + 1) * N_HEADS, :], 0.0)
            o_ref[0, i:i + 1, :] = jnp.sum(blk, axis=0, keepdims=True).astype(o_ref.dtype)


def _attn_sample(q, k_new, v_new, cache_kt, cache_vt, page_table, bias2):
    bsz, t_new, _ = q.shape
    n_pages = page_table.shape[1]
    npg = SAMPLE_PAGES_PER_STEP
    steps = n_pages // npg
    rows = t_new * N_HEADS
    bias_col = jnp.tile(bias2, t_new).reshape(rows, 1)
    tok = pl.BlockSpec((1, t_new, D_MODEL), lambda b, j, pt: (b, 0, 0))

    def page_spec(r):
        return pl.BlockSpec((1, D_MODEL, PAGE_SIZE), lambda b, j, pt: (pt[b, n_pages - 1 - j * npg - r], 0, 0))

    grid_spec = pltpu.PrefetchScalarGridSpec(
        num_scalar_prefetch=1,
        grid=(bsz, steps),
        in_specs=[tok, tok, tok, pl.BlockSpec((rows, 1), lambda b, j, pt: (0, 0))]
        + [page_spec(r) for r in range(npg)] + [page_spec(r) for r in range(npg)],
        out_specs=tok,
        scratch_shapes=[
            pltpu.VMEM((rows, D_MODEL), BF16),
            pltpu.VMEM((rows, D_MODEL), F32),
            pltpu.VMEM((rows, 1), F32),
        ],
    )
    return pl.pallas_call(
        _attn_sample_kernel,
        grid_spec=grid_spec,
        out_shape=jax.ShapeDtypeStruct((bsz, t_new, D_MODEL), BF16),
        compiler_params=_cparams("arbitrary", "arbitrary"),
        name="attn_sample",
    )(page_table, q, k_new, v_new, bias_col, *([cache_kt] * npg), *([cache_vt] * npg))


def _router_params(w_coarse, b_coarse, w_fine, b_fine):
    wf = jnp.transpose(w_fine, (1, 0, 2)).reshape(D_MODEL, N_EXPERTS)
    w = jnp.concatenate([wf, w_coarse, jnp.zeros((D_MODEL, LANES - N_EXPERTS - N_GROUPS), F32)], axis=1)
    b = jnp.concatenate([b_fine.reshape(-1), b_coarse, jnp.zeros((LANES - N_EXPERTS - N_GROUPS,), F32)])
    return w, b.reshape(1, LANES)


def _ffn(hn, res, mw, g_final, tm, final_norm):
    w_r, b_r, wg, wu, wd = mw
    comb = _router(hn, w_r, b_r, tm)
    return _moe(hn, comb, wg, wu, wd, res, g_final, tm, final_norm)


def kernel(x_prompt, x_sample, state_conv, cache_k, cache_v, page_table, norm_mix, norm_ffn, norm_final,
           conv_w_pw1, conv_b_pw1, conv_w_dw, conv_b_dw, conv_ln_g, conv_ln_b, conv_w_pw2, conv_b_pw2,
           sb_w_qkv, sb_w_o, sb_logit_bias, moe_w_coarse, moe_b_coarse, moe_w_fine, moe_b_fine,
           moe_w_gate, moe_w_up, moe_w_down):
    bsz, t_len, _ = x_prompt.shape
    dbsz, t_new, _ = x_sample.shape
    n_p = bsz * t_len
    n_s = dbsz * t_new
    tm_p, tm_s = 512, n_s
    vec = lambda a: a.reshape(1, -1)
    g_fin = vec(norm_final)

    xp = x_prompt.reshape(n_p, D_MODEL)
    xs = x_sample.reshape(n_s, D_MODEL)

    moe_w = []
    for i in range(2):
        w_r, b_r = _router_params(moe_w_coarse[i], moe_b_coarse[i], moe_w_fine[i], moe_b_fine[i])
        moe_w.append((w_r, b_r, moe_w_gate[i].astype(BF16), moe_w_up[i].astype(BF16), moe_w_down[i].astype(BF16)))

    w1 = conv_w_pw1[0].astype(BF16)
    w2 = conv_w_pw2[0].astype(BF16)
    up = _pw1_glu(xp, vec(norm_mix[0]), w1, vec(conv_b_pw1[0]), tm_p)
    us = _pw1_glu(xs, vec(norm_mix[0]), w1, vec(conv_b_pw1[0]), tm_s)
    up3 = up.reshape(bsz, t_len, D_MODEL)
    full_s = jnp.concatenate([state_conv[0], us.reshape(dbsz, t_new, D_MODEL)], axis=1)
    conv_state_p = up3[:, t_len - (CONV_WIDTH - 1):, :][None]
    conv_state_s = full_s[:, t_new:, :][None]
    cp = _conv_prompt(up3, conv_w_dw[0], vec(conv_b_dw[0]), 256).reshape(n_p, D_MODEL)
    cs = _conv_sample(full_s, conv_w_dw[0], vec(conv_b_dw[0]), 8).reshape(n_s, D_MODEL)
    ln = (vec(conv_ln_g[0]), vec(conv_ln_b[0]), vec(norm_ffn[0]))
    yp, hp = _proj(cp, w2, vec(conv_b_pw2[0]), xp, *ln, tm_p, True)
    ys, hs = _proj(cs, w2, vec(conv_b_pw2[0]), xs, *ln, tm_s, True)
    yp = _ffn(hp, yp, moe_w[0], g_fin, tm_p, False)
    ys = _ffn(hs, ys, moe_w[0], g_fin, tm_s, False)

    wqkv = sb_w_qkv[0].astype(BF16)
    wo = sb_w_o[0].astype(BF16)
    bias2 = sb_logit_bias[0].astype(F32) * LOG2E
    qp, ktp, vtp, ktb, vtb = _qkv_t(yp.reshape(bsz, t_len, D_MODEL), vec(norm_mix[1]), wqkv[:, :D_MODEL],
                                    wqkv[:, D_MODEL:2 * D_MODEL].T, wqkv[:, 2 * D_MODEL:].T, tm_p)
    qs, ks, vs = _qkv(ys, vec(norm_mix[1]), wqkv, tm_s)
    op = _attn_prompt(qp, ktb, vtb, bias2)
    n_pool = cache_k.shape[1]
    cache_kt = jnp.transpose(cache_k[0], (0, 2, 3, 1)).reshape(n_pool, D_MODEL, PAGE_SIZE)
    cache_vt = jnp.transpose(cache_v[0], (0, 2, 3, 1)).reshape(n_pool, D_MODEL, PAGE_SIZE)
    os_ = _attn_sample(qs.reshape(dbsz, t_new, D_MODEL), ks.reshape(dbsz, t_new, D_MODEL),
                       vs.reshape(dbsz, t_new, D_MODEL), cache_kt, cache_vt,
                       page_table, bias2).reshape(n_s, D_MODEL)
    zero_b = jnp.zeros((1, D_MODEL), F32)
    nf1 = vec(norm_ffn[1])
    yp, hp = _proj(op, wo, zero_b, yp, nf1, nf1, nf1, tm_p, False)
    ys, hs = _proj(os_, wo, zero_b, ys, nf1, nf1, nf1, tm_s, False)
    yp = _ffn(hp, yp, moe_w[1], g_fin, tm_p, True)
    ys = _ffn(hs, ys, moe_w[1], g_fin, tm_s, True)

    def rows_p(xt):
        return jnp.transpose(xt.reshape(bsz, N_HEADS, HEAD_DIM, t_len), (0, 3, 1, 2))[None]

    hshape_s = (1, dbsz, t_new, N_HEADS, HEAD_DIM)
    return (yp.reshape(bsz, t_len, D_MODEL), ys.reshape(dbsz, t_new, D_MODEL), conv_state_p, conv_state_s,
            rows_p(ktp), rows_p(vtp), ks.reshape(hshape_s), vs.reshape(hshape_s))
```

```python
import functools

import jax
import jax.numpy as jnp
from jax import lax
from jax.experimental import pallas as pl
from jax.experimental.pallas import tpu as pltpu

D_MODEL = 1024
CONV_WIDTH = 31
N_HEADS = 16
HEAD_DIM = D_MODEL // N_HEADS
N_GROUPS = 4
EXPERTS_PER_GROUP = 8
N_EXPERTS = N_GROUPS * EXPERTS_PER_GROUP
D_EXPERT = D_MODEL // 4
PAGE_SIZE = 128
NORM_EPS = 1e-6
LN_EPS = 1e-5

LANES = 128
SUBLANES = 8
HIST_PAD = 32
VMEM_LIMIT = 56 * 1024 * 1024

F32 = jnp.float32
BF16 = jnp.bfloat16


def _cparams(*sem):
    return pltpu.CompilerParams(dimension_semantics=sem, vmem_limit_bytes=VMEM_LIMIT)


def _rms(x, g):
    return x * lax.rsqrt(jnp.mean(x * x, axis=-1, keepdims=True) + NORM_EPS) * g


LOG2E = 1.4426950408889634
POW2_CLAMP = 64.0


def _softplus2(z2):
    return jnp.where(z2 > POW2_CLAMP, z2, jnp.log2(1.0 + jnp.exp2(jnp.minimum(z2, POW2_CLAMP))))


def _pw1_glu_kernel(x_ref, g_ref, w_ref, b_ref, u_ref):
    h = _rms(x_ref[...], g_ref[...]).astype(BF16)
    a = jnp.dot(h, w_ref[...], preferred_element_type=F32) + b_ref[...]
    u_ref[...] = a[:, :D_MODEL] * jax.nn.sigmoid(a[:, D_MODEL:])


def _pw1_glu(x, g, w_bf, b, tm):
    n = x.shape[0]
    return pl.pallas_call(
        _pw1_glu_kernel,
        grid=(n // tm,),
        in_specs=[
            pl.BlockSpec((tm, D_MODEL), lambda i: (i, 0)),
            pl.BlockSpec((1, D_MODEL), lambda i: (0, 0)),
            pl.BlockSpec((D_MODEL, 2 * D_MODEL), lambda i: (0, 0)),
            pl.BlockSpec((1, 2 * D_MODEL), lambda i: (0, 0)),
        ],
        out_specs=pl.BlockSpec((tm, D_MODEL), lambda i: (i, 0)),
        out_shape=jax.ShapeDtypeStruct((n, D_MODEL), F32),
        compiler_params=_cparams("arbitrary"),
        name="pw1_glu",
    )(x, g, w_bf, b)


Q_SCALE2 = (HEAD_DIM ** -0.5) * LOG2E


def _qkv_kernel(x_ref, g_ref, w_ref, q_ref, k_ref, v_ref):
    h = _rms(x_ref[...], g_ref[...]).astype(BF16)
    a = jnp.dot(h, w_ref[...], preferred_element_type=F32)
    q_ref[...] = (a[:, :D_MODEL] * Q_SCALE2).astype(BF16)
    k_ref[...] = a[:, D_MODEL:2 * D_MODEL]
    v_ref[...] = a[:, 2 * D_MODEL:]


def _qkv(x, g, w_bf, tm):
    n = x.shape[0]
    row = pl.BlockSpec((tm, D_MODEL), lambda i: (i, 0))
    return pl.pallas_call(
        _qkv_kernel,
        grid=(n // tm,),
        in_specs=[
            row,
            pl.BlockSpec((1, D_MODEL), lambda i: (0, 0)),
            pl.BlockSpec((D_MODEL, 3 * D_MODEL), lambda i: (0, 0)),
        ],
        out_specs=[row, row, row],
        out_shape=[
            jax.ShapeDtypeStruct((n, D_MODEL), BF16),
            jax.ShapeDtypeStruct((n, D_MODEL), F32),
            jax.ShapeDtypeStruct((n, D_MODEL), F32),
        ],
        compiler_params=_cparams("arbitrary"),
        name="qkv",
    )(x, g, w_bf)


def _qkv_t_kernel(x_ref, g_ref, wq_ref, wkt_ref, wvt_ref, q_ref, kt_ref, vt_ref, ktb_ref, vtb_ref):
    h = _rms(x_ref[0], g_ref[...]).astype(BF16)
    q_ref[0] = (jnp.dot(h, wq_ref[...], preferred_element_type=F32) * Q_SCALE2).astype(BF16)
    nt = (((1,), (1,)), ((), ()))
    kt = lax.dot_general(wkt_ref[...], h, nt, preferred_element_type=F32)
    vt = lax.dot_general(wvt_ref[...], h, nt, preferred_element_type=F32)
    kt_ref[0] = kt
    vt_ref[0] = vt
    ktb_ref[0] = kt.astype(BF16)
    vtb_ref[0] = vt.astype(BF16)


def _qkv_t(x3, g, wq_bf, wkt_bf, wvt_bf, tm):
    bsz, t_len, _ = x3.shape
    row = pl.BlockSpec((1, tm, D_MODEL), lambda b, t: (b, t, 0))
    col = pl.BlockSpec((1, D_MODEL, tm), lambda b, t: (b, 0, t))
    wspec = pl.BlockSpec((D_MODEL, D_MODEL), lambda b, t: (0, 0))
    tshape = (bsz, D_MODEL, t_len)
    return pl.pallas_call(
        _qkv_t_kernel,
        grid=(bsz, t_len // tm),
        in_specs=[row, pl.BlockSpec((1, D_MODEL), lambda b, t: (0, 0)), wspec, wspec, wspec],
        out_specs=[row, col, col, col, col],
        out_shape=[
            jax.ShapeDtypeStruct((bsz, t_len, D_MODEL), BF16),
            jax.ShapeDtypeStruct(tshape, F32),
            jax.ShapeDtypeStruct(tshape, F32),
            jax.ShapeDtypeStruct(tshape, BF16),
            jax.ShapeDtypeStruct(tshape, BF16),
        ],
        compiler_params=_cparams("arbitrary", "arbitrary"),
        name="qkv_t",
    )(x3, g, wq_bf, wkt_bf, wvt_bf)


CONV_ROWS = 16


def _conv_prompt_kernel(u_ref, w_ref, b_ref, c_ref, sh_ref):
    tt = u_ref.shape[1]
    rows = HIST_PAD + tt
    t = pl.program_id(1)

    @pl.when(t == 0)
    def _():
        sh_ref[0, 0:HIST_PAD, :] = jnp.zeros((HIST_PAD, D_MODEL), F32)

    sh_ref[0, HIST_PAD:rows, :] = u_ref[0]
    for s in range(1, SUBLANES):
        sh_ref[s] = pltpu.roll(sh_ref[0], rows - s, axis=0)
    first = HIST_PAD - (CONV_WIDTH - 1)
    bias = b_ref[...]

    def body(r, carry):
        r0 = pl.multiple_of(r * CONV_ROWS, CONV_ROWS)
        accs = [jnp.zeros((SUBLANES, D_MODEL), F32) + bias for _ in range(CONV_ROWS // SUBLANES)]
        for j in range(CONV_WIDTH):
            s, a = (first + j) % SUBLANES, (first + j) // SUBLANES
            w = w_ref[j * SUBLANES:(j + 1) * SUBLANES, :]
            for i in range(len(accs)):
                accs[i] = accs[i] + sh_ref[s, pl.ds(r0 + (a + i) * SUBLANES, SUBLANES), :] * w
        for i, acc in enumerate(accs):
            c_ref[0, pl.ds(r0 + i * SUBLANES, SUBLANES), :] = acc
        return carry

    lax.fori_loop(0, tt // CONV_ROWS, body, 0)
    sh_ref[0, 0:HIST_PAD, :] = sh_ref[0, tt:rows, :]


def _conv_prompt(u, w_dw, b_dw, tt):
    bsz, t_len, _ = u.shape
    blk = pl.BlockSpec((1, tt, D_MODEL), lambda b, t: (b, t, 0))
    w_rep = jnp.repeat(w_dw, SUBLANES, axis=0)
    return pl.pallas_call(
        _conv_prompt_kernel,
        grid=(bsz, t_len // tt),
        in_specs=[
            blk,
            pl.BlockSpec((CONV_WIDTH * SUBLANES, D_MODEL), lambda b, t: (0, 0)),
            pl.BlockSpec((1, D_MODEL), lambda b, t: (0, 0)),
        ],
        out_specs=blk,
        out_shape=jax.ShapeDtypeStruct(u.shape, F32),
        scratch_shapes=[pltpu.VMEM((SUBLANES, HIST_PAD + tt, D_MODEL), F32)],
        compiler_params=_cparams("arbitrary", "arbitrary"),
        name="conv_prompt",
    )(u, w_rep, b_dw)


def _conv_sample_kernel(full_ref, w_ref, b_ref, c_ref):
    bb = full_ref.shape[0]
    t_new = c_ref.shape[1]
    w = w_ref[...]
    bias = b_ref[...]
    for b in range(bb):
        for i in range(t_new):
            win = full_ref[b, i:i + CONV_WIDTH, :]
            c_ref[b, i:i + 1, :] = jnp.sum(win * w, axis=0, keepdims=True) + bias


def _conv_sample(full, w_dw, b_dw, bb):
    bsz, rows, _ = full.shape
    t_new = rows - (CONV_WIDTH - 1)
    return pl.pallas_call(
        _conv_sample_kernel,
        grid=(bsz // bb,),
        in_specs=[
            pl.BlockSpec((bb, rows, D_MODEL), lambda i: (i, 0, 0)),
            pl.BlockSpec((CONV_WIDTH, D_MODEL), lambda i: (0, 0)),
            pl.BlockSpec((1, D_MODEL), lambda i: (0, 0)),
        ],
        out_specs=pl.BlockSpec((bb, t_new, D_MODEL), lambda i: (i, 0, 0)),
        out_shape=jax.ShapeDtypeStruct((bsz, t_new, D_MODEL), F32),
        compiler_params=_cparams("arbitrary"),
        name="conv_sample",
    )(full, w_dw, b_dw)


def _proj_kernel(a_ref, w_ref, b_ref, res_ref, lng_ref, lnb_ref, gf_ref, y_ref, hn_ref, *, pre_ln):
    a = a_ref[...]
    if pre_ln:
        a = a.astype(F32)
        mu = jnp.mean(a, axis=-1, keepdims=True)
        ac = a - mu
        var = jnp.mean(ac * ac, axis=-1, keepdims=True)
        a = ac * lax.rsqrt(var + LN_EPS) * lng_ref[...] + lnb_ref[...]
        a = a * jax.nn.sigmoid(a)
    y = res_ref[...] + jnp.dot(a.astype(BF16), w_ref[...], preferred_element_type=F32) + b_ref[...]
    y_ref[...] = y
    hn_ref[...] = _rms(y, gf_ref[...])


def _proj(a, w_bf, b, res, ln_g, ln_b, g_ffn, tm, pre_ln):
    n = a.shape[0]
    row = pl.BlockSpec((tm, D_MODEL), lambda i: (i, 0))
    vec = pl.BlockSpec((1, D_MODEL), lambda i: (0, 0))
    return pl.pallas_call(
        functools.partial(_proj_kernel, pre_ln=pre_ln),
        grid=(n // tm,),
        in_specs=[row, pl.BlockSpec((D_MODEL, D_MODEL), lambda i: (0, 0)), vec, row, vec, vec, vec],
        out_specs=[row, row],
        out_shape=[jax.ShapeDtypeStruct((n, D_MODEL), F32), jax.ShapeDtypeStruct((n, D_MODEL), F32)],
        compiler_params=_cparams("arbitrary"),
        name="proj_ln" if pre_ln else "proj",
    )(a, w_bf, b, res, ln_g, ln_b, g_ffn)


def _router_kernel(hn_ref, w_ref, b_ref, idx_ref, gate_ref, cnt_ref):
    logits = jnp.dot(hn_ref[...], w_ref[...], preferred_element_type=F32,
                     precision=lax.Precision.HIGHEST) + b_ref[...]
    lane = lax.broadcasted_iota(jnp.int32, logits.shape, 1)
    neg = jnp.float32(-jnp.inf)
    cmask = (lane >= N_EXPERTS) & (lane < N_EXPERTS + N_GROUPS)
    cl = jnp.where(cmask, logits, neg)
    cmax = jnp.max(cl, axis=-1, keepdims=True)
    g_lane = jnp.min(jnp.where(cl == cmax, lane, LANES), axis=-1, keepdims=True)
    g_top = 1.0 / jnp.sum(jnp.exp(cl - cmax), axis=-1, keepdims=True)
    lo = (g_lane - N_EXPERTS) * EXPERTS_PER_GROUP
    fmask = (lane >= lo) & (lane < lo + EXPERTS_PER_GROUP)
    fl = jnp.where(fmask, logits, neg)
    m1 = jnp.max(fl, axis=-1, keepdims=True)
    i1 = jnp.min(jnp.where(fl == m1, lane, LANES), axis=-1, keepdims=True)
    fl2 = jnp.where(lane == i1, neg, fl)
    m2 = jnp.max(fl2, axis=-1, keepdims=True)
    i2 = jnp.min(jnp.where(fl2 == m2, lane, LANES), axis=-1, keepdims=True)
    r = jnp.exp(m2 - m1)
    w1 = 1.0 / (1.0 + r)
    w2 = r * w1
    tm = logits.shape[0]
    onehot = jnp.where((lane == i1) | (lane == i2), 1.0, 0.0)
    tr = lax.broadcasted_iota(jnp.int32, (tm, tm), 0)
    tc = lax.broadcasted_iota(jnp.int32, (tm, tm), 1)
    before = jnp.where(tc < tr, 1.0, 0.0).astype(BF16)

    @pl.when(pl.program_id(0) == 0)
    def _():
        cnt_ref[...] = jnp.zeros_like(cnt_ref)

    prior = jnp.dot(before, onehot.astype(BF16), preferred_element_type=F32) + cnt_ref[...]
    rank1 = jnp.sum(jnp.where(lane == i1, prior, 0.0), axis=-1, keepdims=True)
    rank2 = jnp.sum(jnp.where(lane == i2, prior, 0.0), axis=-1, keepdims=True)
    cnt_ref[...] += jnp.sum(onehot, axis=0, keepdims=True)
    idx = jnp.where(lane == 0, i1, jnp.where(lane == 1, i2, jnp.where(
        lane == 2, rank1.astype(jnp.int32), jnp.where(lane == 3, rank2.astype(jnp.int32), 0))))
    idx_ref[...] = idx
    gate_ref[...] = g_top * jnp.where(lane == 0, w1, jnp.where(lane == 1, w2, 0.0))


def _router(hn, w_r, b_r, tm):
    n = hn.shape[0]
    row = pl.BlockSpec((tm, LANES), lambda i: (i, 0))
    return pl.pallas_call(
        _router_kernel,
        grid=(n // tm,),
        in_specs=[
            pl.BlockSpec((tm, D_MODEL), lambda i: (i, 0)),
            pl.BlockSpec((D_MODEL, LANES), lambda i: (0, 0)),
            pl.BlockSpec((1, LANES), lambda i: (0, 0)),
        ],
        out_specs=[row, row, pl.BlockSpec((1, LANES), lambda i: (0, 0))],
        out_shape=[jax.ShapeDtypeStruct((n, LANES), jnp.int32), jax.ShapeDtypeStruct((n, LANES), F32),
                   jax.ShapeDtypeStruct((1, LANES), F32)],
        compiler_params=_cparams("arbitrary"),
        name="router",
    )(hn, w_r, b_r)


TOP_K = 2


def _row_copy_wait(src, dst, sem, rows):
    pltpu.make_async_copy(src.at[pl.ds(0, rows), :], dst.at[pl.ds(0, rows), :], sem).wait()


def _dispatch_kernel(pos_ref, x_ref, xs_init_ref, xs_ref, sem):
    del xs_init_ref
    tm = x_ref.shape[0]
    base = pl.program_id(0) * tm

    def issue(t, carry):
        for k in range(TOP_K):
            p = pos_ref[TOP_K * (base + t) + k]
            pltpu.make_async_copy(x_ref.at[pl.ds(t, 1), :], xs_ref.at[pl.ds(p, 1), :], sem).start()
        return carry

    lax.fori_loop(0, tm, issue, 0, unroll=4)
    for k in range(TOP_K):
        _row_copy_wait(x_ref, xs_ref, sem, tm)


def _dispatch(x, pos, n_rows, tm):
    n = x.shape[0]
    grid_spec = pltpu.PrefetchScalarGridSpec(
        num_scalar_prefetch=1,
        grid=(n // tm,),
        in_specs=[pl.BlockSpec((tm, D_MODEL), lambda i, pos: (i, 0)), pl.BlockSpec(memory_space=pl.ANY)],
        out_specs=pl.BlockSpec(memory_space=pl.ANY),
        scratch_shapes=[pltpu.SemaphoreType.DMA(())],
    )
    return pl.pallas_call(
        _dispatch_kernel,
        grid_spec=grid_spec,
        out_shape=jax.ShapeDtypeStruct((n_rows, D_MODEL), F32),
        input_output_aliases={2: 0},
        compiler_params=_cparams("arbitrary"),
        name="moe_dispatch",
    )(pos, x, jnp.zeros((n_rows, D_MODEL), F32))


def _moe_sorted_kernel(te_ref, xs_ref, wg_ref, wu_ref, wd_ref, ys_ref, wgb_ref, wub_ref, wdb_ref):
    i = pl.program_id(0)

    @pl.when((i == 0) | (te_ref[i] != te_ref[jnp.maximum(i - 1, 0)]))
    def _():
        wgb_ref[...] = wg_ref[0].astype(BF16)
        wub_ref[...] = wu_ref[0].astype(BF16)
        wdb_ref[...] = wd_ref[0].astype(BF16)

    x = xs_ref[...].astype(BF16)
    hg = jnp.dot(x, wgb_ref[...], preferred_element_type=F32)
    hu = jnp.dot(x, wub_ref[...], preferred_element_type=F32)
    h = hg * jax.nn.sigmoid(hg) * hu
    ys_ref[...] = jnp.dot(h.astype(BF16), wdb_ref[...], preferred_element_type=F32)


def _moe_sorted(xs, tile_expert, w_gate, w_up, w_down, tm):
    n_rows = xs.shape[0]
    row = pl.BlockSpec((tm, D_MODEL), lambda i, te: (i, 0))
    grid_spec = pltpu.PrefetchScalarGridSpec(
        num_scalar_prefetch=1,
        grid=(n_rows // tm,),
        in_specs=[
            row,
            pl.BlockSpec((1, D_MODEL, D_EXPERT), lambda i, te: (te[i], 0, 0)),
            pl.BlockSpec((1, D_MODEL, D_EXPERT), lambda i, te: (te[i], 0, 0)),
            pl.BlockSpec((1, D_EXPERT, D_MODEL), lambda i, te: (te[i], 0, 0)),
        ],
        out_specs=row,
        scratch_shapes=[pltpu.VMEM((D_MODEL, D_EXPERT), BF16), pltpu.VMEM((D_MODEL, D_EXPERT), BF16),
                        pltpu.VMEM((D_EXPERT, D_MODEL), BF16)],
    )
    return pl.pallas_call(
        _moe_sorted_kernel,
        grid_spec=grid_spec,
        out_shape=jax.ShapeDtypeStruct((n_rows, D_MODEL), F32),
        compiler_params=_cparams("arbitrary"),
        name="moe_experts",
    )(tile_expert, xs, w_gate, w_up, w_down)


def _combine_kernel(pos_ref, res_ref, gate_ref, gfin_ref, ys_ref, y_ref, buf_ref, sem, *, final_norm):
    tm = res_ref.shape[0]
    base = pl.program_id(0) * tm

    def issue(t, carry):
        for k in range(TOP_K):
            p = pos_ref[TOP_K * (base + t) + k]
            pltpu.make_async_copy(ys_ref.at[pl.ds(p, 1), :], buf_ref.at[k, pl.ds(t, 1), :], sem).start()
        return carry

    lax.fori_loop(0, tm, issue, 0, unroll=4)
    for k in range(TOP_K):
        _row_copy_wait(ys_ref, buf_ref.at[k], sem, tm)
    g = gate_ref[...]
    y = res_ref[...] + g[:, 0:1] * buf_ref[0] + g[:, 1:2] * buf_ref[1]
    if final_norm:
        y = _rms(y, gfin_ref[...])
    y_ref[...] = y


def _combine(ys, pos, res, gate, g_final, tm, final_norm):
    n = res.shape[0]
    row = pl.BlockSpec((tm, D_MODEL), lambda i, pos: (i, 0))
    grid_spec = pltpu.PrefetchScalarGridSpec(
        num_scalar_prefetch=1,
        grid=(n // tm,),
        in_specs=[row, pl.BlockSpec((tm, LANES), lambda i, pos: (i, 0)),
                  pl.BlockSpec((1, D_MODEL), lambda i, pos: (0, 0)), pl.BlockSpec(memory_space=pl.ANY)],
        out_specs=row,
        scratch_shapes=[pltpu.VMEM((TOP_K, tm, D_MODEL), F32), pltpu.SemaphoreType.DMA(())],
    )
    return pl.pallas_call(
        functools.partial(_combine_kernel, final_norm=final_norm),
        grid_spec=grid_spec,
        out_shape=jax.ShapeDtypeStruct((n, D_MODEL), F32),
        compiler_params=_cparams("arbitrary"),
        name="moe_combine_final" if final_norm else "moe_combine",
    )(pos, res, gate, g_final, ys)


def _route_plan(idx, cnt, tm_moe, n_tiles):
    experts = idx[:, 0:TOP_K]
    ranks = idx[:, TOP_K:2 * TOP_K]
    counts = cnt[0, :N_EXPERTS].astype(jnp.int32)
    padded = (counts + tm_moe - 1) // tm_moe * tm_moe
    ends = jnp.cumsum(padded)
    starts = ends - padded
    pos = (starts[experts] + ranks).reshape(-1)
    tile_start = jnp.arange(n_tiles, dtype=jnp.int32) * tm_moe
    tile_expert = jnp.minimum(jnp.searchsorted(ends, tile_start, side="right"), N_EXPERTS - 1).astype(jnp.int32)
    return pos, tile_expert


ATT_BLK = 128
ATT_TQ = 512


NT_DIMS = (((1,), (1,)), ((), ()))


def _neg_suffix_ones(n):
    r = lax.broadcasted_iota(jnp.int32, (n, n), 0)
    c = lax.broadcasted_iota(jnp.int32, (n, n), 1)
    return jnp.where(r >= c, -1.0, 0.0).astype(BF16)


def _sb_weights(z2, neg_ge, run, mask):
    sp = _softplus2(z2)
    if mask is not None:
        sp = jnp.where(mask, sp, 0.0)
    c = jnp.dot(sp.astype(BF16), neg_ge, preferred_element_type=F32)
    a = jnp.exp2(z2 + c + run)
    if mask is not None:
        a = jnp.where(mask, a, 0.0)
    return a, run + c[:, 0:1]


def _attn_prompt_kernel(bias2_ref, q_ref, kt_ref, vt_ref, o_ref):
    hp = pl.program_id(1)
    qi = pl.program_id(2)
    nsub = ATT_TQ // ATT_BLK
    neg_ge = _neg_suffix_ones(ATT_BLK)
    lane = lax.broadcasted_iota(jnp.int32, (ATT_BLK, LANES), 1)
    row = lax.broadcasted_iota(jnp.int32, (ATT_BLK, ATT_BLK), 0)
    col = lax.broadcasted_iota(jnp.int32, (ATT_BLK, ATT_BLK), 1)
    diag_mask = col < row
    bias2 = [bias2_ref[2 * hp + h] for h in range(2)]

    chains = []
    for sq in range(nsub):
        q = q_ref[0, sq * ATT_BLK:(sq + 1) * ATT_BLK, :]
        for h in range(2):
            head = (lane >= h * HEAD_DIM) & (lane < (h + 1) * HEAD_DIM)
            chains.append((jnp.where(head, q, jnp.zeros_like(q)), bias2[h]))
    nch = len(chains)

    def chunk_pass(c0, nblks, runs, diagonal):
        kt_c = kt_ref[0, :, pl.ds(c0, ATT_TQ)]
        vt_c = vt_ref[0, :, pl.ds(c0, ATT_TQ)]
        blk = lambda x, b: x[:, b * ATT_BLK:(b + 1) * ATT_BLK]
        zs = [jnp.dot(qh, kt_c[:, :nb * ATT_BLK], preferred_element_type=F32) + bias
              for (qh, bias), nb in zip(chains, nblks)]
        runs = list(runs)
        parts = [[None] * nb for nb in nblks]
        for b in range(max(nblks) - 1, -1, -1):
            live = [n for n in range(nch) if nblks[n] > b]
            masked = {n: diagonal and b == nblks[n] - 1 for n in live}
            sps = {}
            for n in live:
                sp = _softplus2(blk(zs[n], b))
                if masked[n]:
                    sp = jnp.where(diag_mask, sp, 0.0)
                sps[n] = sp.astype(BF16)
            cs = {n: jnp.dot(sps[n], neg_ge, preferred_element_type=F32) for n in live}
            for n in live:
                a = jnp.exp2(blk(zs[n], b) + cs[n] + runs[n])
                if masked[n]:
                    a = jnp.where(diag_mask, a, 0.0)
                parts[n][b] = a.astype(BF16)
                runs[n] = runs[n] + cs[n][:, 0:1]
        outs = []
        for n in range(nch):
            a_c = parts[n][0] if nblks[n] == 1 else jnp.concatenate(parts[n], axis=1)
            outs.append(lax.dot_general(a_c, vt_c[:, :nblks[n] * ATT_BLK], NT_DIMS, preferred_element_type=F32))
        return outs, runs

    outs, runs = chunk_pass(pl.multiple_of(qi * ATT_TQ, ATT_TQ), [n // 2 + 1 for n in range(nch)],
                            [jnp.zeros((ATT_BLK, 1), F32)] * nch, True)
    state = [x for n in range(nch) for x in (outs[n], runs[n])]

    def body(it, st):
        outs, runs = chunk_pass(pl.multiple_of((qi - 1 - it) * ATT_TQ, ATT_TQ), [nsub] * nch,
                                [st[2 * n + 1] for n in range(nch)], False)
        return tuple(x for n in range(nch) for x in (st[2 * n] + outs[n], runs[n]))

    st = lax.fori_loop(0, qi, body, tuple(state))
    for sq in range(nsub):
        o_ref[0, sq * ATT_BLK:(sq + 1) * ATT_BLK, :] = jnp.where(
            lane < HEAD_DIM, st[4 * sq], st[4 * sq + 2]).astype(o_ref.dtype)


def _attn_prompt(q3, ktb, vtb, bias2):
    bsz, t_len, _ = q3.shape
    qblk = pl.BlockSpec((1, ATT_TQ, LANES), lambda b, hp, qi: (b, qi, hp))
    kvblk = pl.BlockSpec((1, LANES, t_len), lambda b, hp, qi: (b, hp, 0))
    o = pl.pallas_call(
        _attn_prompt_kernel,
        grid=(bsz, D_MODEL // LANES, t_len // ATT_TQ),
        in_specs=[pl.BlockSpec(memory_space=pltpu.SMEM), qblk, kvblk, kvblk],
        out_specs=qblk,
        out_shape=jax.ShapeDtypeStruct((bsz, t_len, D_MODEL), BF16),
        compiler_params=_cparams("arbitrary", "arbitrary", "arbitrary"),
        name="attn_prompt",
    )(bias2, q3, ktb, vtb)
    return o.reshape(bsz * t_len, D_MODEL)


SAMPLE_PAGES_PER_STEP = 8


def _attn_sample_kernel(pt_ref, q_ref, kn_ref, vn_ref, bias_ref, *refs):
    npg = SAMPLE_PAGES_PER_STEP
    k_refs = refs[:npg]
    v_refs = refs[npg:2 * npg]
    o_ref = refs[2 * npg]
    qbd_ref, acc_ref, carry_ref = refs[2 * npg + 1:]
    j = pl.program_id(1)
    t_new = q_ref.shape[1]
    rows = t_new * N_HEADS
    lane = lax.broadcasted_iota(jnp.int32, (N_HEADS, D_MODEL), 1)
    sub = lax.broadcasted_iota(jnp.int32, (N_HEADS, D_MODEL), 0)
    head_lanes = (lane >= sub * HEAD_DIM) & (lane < (sub + 1) * HEAD_DIM)
    bias = bias_ref[...]

    @pl.when(j == 0)
    def _():
        q = q_ref[0].astype(F32)
        qbd = jnp.concatenate(
            [jnp.where(head_lanes, jnp.broadcast_to(q[i:i + 1, :], (N_HEADS, D_MODEL)), 0.0) for i in range(t_new)],
            axis=0)
        qbd_ref[...] = qbd.astype(BF16)
        qi = lax.broadcasted_iota(jnp.int32, (rows, 1), 0) // N_HEADS
        kn = kn_ref[0]
        vn = vn_ref[0]
        carry = jnp.zeros((rows, 1), F32)
        acc = jnp.zeros((rows, D_MODEL), F32)
        for m in range(t_new - 1, -1, -1):
            valid = qi > m
            z2 = jnp.sum(qbd * kn[m:m + 1, :], axis=-1, keepdims=True) + bias
            sp = jnp.where(valid, _softplus2(z2), 0.0)
            a = jnp.where(valid, jnp.exp2(z2 - sp + carry), 0.0)
            acc = acc + a * vn[m:m + 1, :]
            carry = carry - sp
        acc_ref[...] = acc
        carry_ref[...] = carry

    neg_ge = _neg_suffix_ones(PAGE_SIZE)
    qbd = qbd_ref[...]
    acc = acc_ref[...]
    carry = carry_ref[...]
    for r in range(npg):
        ktp = k_refs[r][0].astype(BF16)
        vtp = v_refs[r][0].astype(BF16)
        z2 = jnp.dot(qbd, ktp, preferred_element_type=F32) + bias
        a, carry = _sb_weights(z2, neg_ge, carry, None)
        acc = acc + lax.dot_general(a.astype(BF16), vtp, NT_DIMS, preferred_element_type=F32)
    acc_ref[...] = acc
    carry_ref[...] = carry

    @pl.when(j == pl.num_programs(1) - 1)
    def _():
        for i in range(t_new):
            blk = jnp.where(head_lanes, acc[i * N_HEADS:(i + 1) * N_HEADS, :], 0.0)
            o_ref[0, i:i + 1, :] = jnp.sum(blk, axis=0, keepdims=True).astype(o_ref.dtype)


def _attn_sample(q, k_new, v_new, cache_kt, cache_vt, page_table, bias2):
    bsz, t_new, _ = q.shape
    n_pages = page_table.shape[1]
    npg = SAMPLE_PAGES_PER_STEP
    steps = n_pages // npg
    rows = t_new * N_HEADS
    bias_col = jnp.tile(bias2, t_new).reshape(rows, 1)
    tok = pl.BlockSpec((1, t_new, D_MODEL), lambda b, j, pt: (b, 0, 0))

    def page_spec(r):
        return pl.BlockSpec((1, D_MODEL, PAGE_SIZE), lambda b, j, pt: (pt[b, n_pages - 1 - j * npg - r], 0, 0))

    grid_spec = pltpu.PrefetchScalarGridSpec(
        num_scalar_prefetch=1,
        grid=(bsz, steps),
        in_specs=[tok, tok, tok, pl.BlockSpec((rows, 1), lambda b, j, pt: (0, 0))]
        + [page_spec(r) for r in range(npg)] + [page_spec(r) for r in range(npg)],
        out_specs=tok,
        scratch_shapes=[
            pltpu.VMEM((rows, D_MODEL), BF16),
            pltpu.VMEM((rows, D_MODEL), F32),
            pltpu.VMEM((rows, 1), F32),
        ],
    )
    return pl.pallas_call(
        _attn_sample_kernel,
        grid_spec=grid_spec,
        out_shape=jax.ShapeDtypeStruct((bsz, t_new, D_MODEL), BF16),
        compiler_params=_cparams("arbitrary", "arbitrary"),
        name="attn_sample",
    )(page_table, q, k_new, v_new, bias_col, *([cache_kt] * npg), *([cache_vt] * npg))


def _router_params(w_coarse, b_coarse, w_fine, b_fine):
    wf = jnp.transpose(w_fine, (1, 0, 2)).reshape(D_MODEL, N_EXPERTS)
    w = jnp.concatenate([wf, w_coarse, jnp.zeros((D_MODEL, LANES - N_EXPERTS - N_GROUPS), F32)], axis=1)
    b = jnp.concatenate([b_fine.reshape(-1), b_coarse, jnp.zeros((LANES - N_EXPERTS - N_GROUPS,), F32)])
    return w, b.reshape(1, LANES)


def _ffn(hn, res, mw, g_final, tm, tm_moe, final_norm):
    w_r, b_r, wg, wu, wd = mw
    n = hn.shape[0]
    idx, gate, cnt = _router(hn, w_r, b_r, tm)
    n_tiles = pl.cdiv(TOP_K * n, tm_moe) + N_EXPERTS
    pos, tile_expert = _route_plan(idx, cnt, tm_moe, n_tiles)
    xs = _dispatch(hn, pos, n_tiles * tm_moe, tm)
    ys = _moe_sorted(xs, tile_expert, wg, wu, wd, tm_moe)
    return _combine(ys, pos, res, gate, g_final, tm, final_norm)


def kernel(x_prompt, x_sample, state_conv, cache_k, cache_v, page_table, norm_mix, norm_ffn, norm_final,
           conv_w_pw1, conv_b_pw1, conv_w_dw, conv_b_dw, conv_ln_g, conv_ln_b, conv_w_pw2, conv_b_pw2,
           sb_w_qkv, sb_w_o, sb_logit_bias, moe_w_coarse, moe_b_coarse, moe_w_fine, moe_b_fine,
           moe_w_gate, moe_w_up, moe_w_down):
    bsz, t_len, _ = x_prompt.shape
    dbsz, t_new, _ = x_sample.shape
    n_p = bsz * t_len
    n_s = dbsz * t_new
    tm_p, tm_s = 512, n_s
    tmm_p, tmm_s = 256, 128
    vec = lambda a: a.reshape(1, -1)
    g_fin = vec(norm_final)

    xp = x_prompt.reshape(n_p, D_MODEL)
    xs = x_sample.reshape(n_s, D_MODEL)

    moe_w = []
    for i in range(2):
        w_r, b_r = _router_params(moe_w_coarse[i], moe_b_coarse[i], moe_w_fine[i], moe_b_fine[i])
        moe_w.append((w_r, b_r, moe_w_gate[i], moe_w_up[i], moe_w_down[i]))

    w1 = conv_w_pw1[0].astype(BF16)
    w2 = conv_w_pw2[0].astype(BF16)
    up = _pw1_glu(xp, vec(norm_mix[0]), w1, vec(conv_b_pw1[0]), tm_p)
    us = _pw1_glu(xs, vec(norm_mix[0]), w1, vec(conv_b_pw1[0]), tm_s)
    up3 = up.reshape(bsz, t_len, D_MODEL)
    full_s = jnp.concatenate([state_conv[0], us.reshape(dbsz, t_new, D_MODEL)], axis=1)
    conv_state_p = up3[:, t_len - (CONV_WIDTH - 1):, :][None]
    conv_state_s = full_s[:, t_new:, :][None]
    cp = _conv_prompt(up3, conv_w_dw[0], vec(conv_b_dw[0]), 256).reshape(n_p, D_MODEL)
    cs = _conv_sample(full_s, conv_w_dw[0], vec(conv_b_dw[0]), 8).reshape(n_s, D_MODEL)
    ln = (vec(conv_ln_g[0]), vec(conv_ln_b[0]), vec(norm_ffn[0]))
    yp, hp = _proj(cp, w2, vec(conv_b_pw2[0]), xp, *ln, tm_p, True)
    ys, hs = _proj(cs, w2, vec(conv_b_pw2[0]), xs, *ln, tm_s, True)
    yp = _ffn(hp, yp, moe_w[0], g_fin, tm_p, tmm_p, False)
    ys = _ffn(hs, ys, moe_w[0], g_fin, tm_s, tmm_s, False)

    wqkv = sb_w_qkv[0].astype(BF16)
    wo = sb_w_o[0].astype(BF16)
    bias2 = sb_logit_bias[0].astype(F32) * LOG2E
    qp, ktp, vtp, ktb, vtb = _qkv_t(yp.reshape(bsz, t_len, D_MODEL), vec(norm_mix[1]), wqkv[:, :D_MODEL],
                                    wqkv[:, D_MODEL:2 * D_MODEL].T, wqkv[:, 2 * D_MODEL:].T, tm_p)
    qs, ks, vs = _qkv(ys, vec(norm_mix[1]), wqkv, tm_s)
    op = _attn_prompt(qp, ktb, vtb, bias2)
    n_pool = cache_k.shape[1]
    cache_kt = jnp.transpose(cache_k[0], (0, 2, 3, 1)).reshape(n_pool, D_MODEL, PAGE_SIZE)
    cache_vt = jnp.transpose(cache_v[0], (0, 2, 3, 1)).reshape(n_pool, D_MODEL, PAGE_SIZE)
    os_ = _attn_sample(qs.reshape(dbsz, t_new, D_MODEL), ks.reshape(dbsz, t_new, D_MODEL),
                       vs.reshape(dbsz, t_new, D_MODEL), cache_kt, cache_vt,
                       page_table, bias2).reshape(n_s, D_MODEL)
    zero_b = jnp.zeros((1, D_MODEL), F32)
    nf1 = vec(norm_ffn[1])
    yp, hp = _proj(op, wo, zero_b, yp, nf1, nf1, nf1, tm_p, False)
    ys, hs = _proj(os_, wo, zero_b, ys, nf1, nf1, nf1, tm_s, False)
    yp = _ffn(hp, yp, moe_w[1], g_fin, tm_p, tmm_p, True)
    ys = _ffn(hs, ys, moe_w[1], g_fin, tm_s, tmm_s, True)

    def rows_p(xt):
        return jnp.transpose(xt.reshape(bsz, N_HEADS, HEAD_DIM, t_len), (0, 3, 1, 2))[None]

    hshape_s = (1, dbsz, t_new, N_HEADS, HEAD_DIM)
    return (yp.reshape(bsz, t_len, D_MODEL), ys.reshape(dbsz, t_new, D_MODEL), conv_state_p, conv_state_s,
            rows_p(ktp), rows_p(vtp), ks.reshape(hshape_s), vs.reshape(hshape_s))
```

```python
import functools

import jax
import jax.numpy as jnp
from jax import lax
from jax.experimental import pallas as pl
from jax.experimental.pallas import tpu as pltpu

D_MODEL = 1024
CONV_WIDTH = 31
N_HEADS = 16
HEAD_DIM = D_MODEL // N_HEADS
N_GROUPS = 4
EXPERTS_PER_GROUP = 8
N_EXPERTS = N_GROUPS * EXPERTS_PER_GROUP
D_EXPERT = D_MODEL // 4
PAGE_SIZE = 128
NORM_EPS = 1e-6
LN_EPS = 1e-5

LANES = 128
SUBLANES = 8
HIST_PAD = 32
VMEM_LIMIT = 56 * 1024 * 1024

F32 = jnp.float32
BF16 = jnp.bfloat16


def _cparams(*sem):
    return pltpu.CompilerParams(dimension_semantics=sem, vmem_limit_bytes=VMEM_LIMIT)


def _rms(x, g):
    return x * lax.rsqrt(jnp.mean(x * x, axis=-1, keepdims=True) + NORM_EPS) * g


LOG2E = 1.4426950408889634
POW2_CLAMP = 64.0


def _softplus2(z2):
    return jnp.where(z2 > POW2_CLAMP, z2, jnp.log2(1.0 + jnp.exp2(jnp.minimum(z2, POW2_CLAMP))))


def _pw1_glu_kernel(x_ref, g_ref, w_ref, b_ref, u_ref):
    h = _rms(x_ref[...], g_ref[...]).astype(BF16)
    a = jnp.dot(h, w_ref[...], preferred_element_type=F32) + b_ref[...]
    u_ref[...] = a[:, :D_MODEL] * jax.nn.sigmoid(a[:, D_MODEL:])


def _pw1_glu(x, g, w_bf, b, tm):
    n = x.shape[0]
    return pl.pallas_call(
        _pw1_glu_kernel,
        grid=(n // tm,),
        in_specs=[
            pl.BlockSpec((tm, D_MODEL), lambda i: (i, 0)),
            pl.BlockSpec((1, D_MODEL), lambda i: (0, 0)),
            pl.BlockSpec((D_MODEL, 2 * D_MODEL), lambda i: (0, 0)),
            pl.BlockSpec((1, 2 * D_MODEL), lambda i: (0, 0)),
        ],
        out_specs=pl.BlockSpec((tm, D_MODEL), lambda i: (i, 0)),
        out_shape=jax.ShapeDtypeStruct((n, D_MODEL), F32),
        compiler_params=_cparams("arbitrary"),
        name="pw1_glu",
    )(x, g, w_bf, b)


Q_SCALE2 = (HEAD_DIM ** -0.5) * LOG2E


def _qkv_kernel(x_ref, g_ref, w_ref, q_ref, k_ref, v_ref):
    h = _rms(x_ref[...], g_ref[...]).astype(BF16)
    a = jnp.dot(h, w_ref[...], preferred_element_type=F32)
    q_ref[...] = (a[:, :D_MODEL] * Q_SCALE2).astype(BF16)
    k_ref[...] = a[:, D_MODEL:2 * D_MODEL]
    v_ref[...] = a[:, 2 * D_MODEL:]


def _qkv(x, g, w_bf, tm):
    n = x.shape[0]
    row = pl.BlockSpec((tm, D_MODEL), lambda i: (i, 0))
    return pl.pallas_call(
        _qkv_kernel,
        grid=(n // tm,),
        in_specs=[
            row,
            pl.BlockSpec((1, D_MODEL), lambda i: (0, 0)),
            pl.BlockSpec((D_MODEL, 3 * D_MODEL), lambda i: (0, 0)),
        ],
        out_specs=[row, row, row],
        out_shape=[
            jax.ShapeDtypeStruct((n, D_MODEL), BF16),
            jax.ShapeDtypeStruct((n, D_MODEL), F32),
            jax.ShapeDtypeStruct((n, D_MODEL), F32),
        ],
        compiler_params=_cparams("arbitrary"),
        name="qkv",
    )(x, g, w_bf)


def _qkv_t_kernel(x_ref, g_ref, wq_ref, wkt_ref, wvt_ref, q_ref, kt_ref, vt_ref, ktb_ref, vtb_ref):
    h = _rms(x_ref[0], g_ref[...]).astype(BF16)
    q_ref[0] = (jnp.dot(h, wq_ref[...], preferred_element_type=F32) * Q_SCALE2).astype(BF16)
    nt = (((1,), (1,)), ((), ()))
    kt = lax.dot_general(wkt_ref[...], h, nt, preferred_element_type=F32)
    vt = lax.dot_general(wvt_ref[...], h, nt, preferred_element_type=F32)
    kt_ref[0] = kt
    vt_ref[0] = vt
    ktb_ref[0] = kt.astype(BF16)
    vtb_ref[0] = vt.astype(BF16)


def _qkv_t(x3, g, wq_bf, wkt_bf, wvt_bf, tm):
    bsz, t_len, _ = x3.shape
    row = pl.BlockSpec((1, tm, D_MODEL), lambda b, t: (b, t, 0))
    col = pl.BlockSpec((1, D_MODEL, tm), lambda b, t: (b, 0, t))
    wspec = pl.BlockSpec((D_MODEL, D_MODEL), lambda b, t: (0, 0))
    tshape = (bsz, D_MODEL, t_len)
    return pl.pallas_call(
        _qkv_t_kernel,
        grid=(bsz, t_len // tm),
        in_specs=[row, pl.BlockSpec((1, D_MODEL), lambda b, t: (0, 0)), wspec, wspec, wspec],
        out_specs=[row, col, col, col, col],
        out_shape=[
            jax.ShapeDtypeStruct((bsz, t_len, D_MODEL), BF16),
            jax.ShapeDtypeStruct(tshape, F32),
            jax.ShapeDtypeStruct(tshape, F32),
            jax.ShapeDtypeStruct(tshape, BF16),
            jax.ShapeDtypeStruct(tshape, BF16),
        ],
        compiler_params=_cparams("arbitrary", "arbitrary"),
        name="qkv_t",
    )(x3, g, wq_bf, wkt_bf, wvt_bf)


CONV_ROWS = 16


def _conv_prompt_kernel(u_ref, w_ref, b_ref, c_ref, sh_ref):
    tt = u_ref.shape[1]
    rows = HIST_PAD + tt
    t = pl.program_id(1)

    @pl.when(t == 0)
    def _():
        sh_ref[0, 0:HIST_PAD, :] = jnp.zeros((HIST_PAD, D_MODEL), F32)

    sh_ref[0, HIST_PAD:rows, :] = u_ref[0]
    for s in range(1, SUBLANES):
        sh_ref[s] = pltpu.roll(sh_ref[0], rows - s, axis=0)
    first = HIST_PAD - (CONV_WIDTH - 1)
    bias = b_ref[...]

    def body(r, carry):
        r0 = pl.multiple_of(r * CONV_ROWS, CONV_ROWS)
        accs = [jnp.zeros((SUBLANES, D_MODEL), F32) + bias for _ in range(CONV_ROWS // SUBLANES)]
        for j in range(CONV_WIDTH):
            s, a = (first + j) % SUBLANES, (first + j) // SUBLANES
            w = w_ref[j * SUBLANES:(j + 1) * SUBLANES, :]
            for i in range(len(accs)):
                accs[i] = accs[i] + sh_ref[s, pl.ds(r0 + (a + i) * SUBLANES, SUBLANES), :] * w
        for i, acc in enumerate(accs):
            c_ref[0, pl.ds(r0 + i * SUBLANES, SUBLANES), :] = acc
        return carry

    lax.fori_loop(0, tt // CONV_ROWS, body, 0)
    sh_ref[0, 0:HIST_PAD, :] = sh_ref[0, tt:rows, :]


def _conv_prompt(u, w_dw, b_dw, tt):
    bsz, t_len, _ = u.shape
    blk = pl.BlockSpec((1, tt, D_MODEL), lambda b, t: (b, t, 0))
    w_rep = jnp.repeat(w_dw, SUBLANES, axis=0)
    return pl.pallas_call(
        _conv_prompt_kernel,
        grid=(bsz, t_len // tt),
        in_specs=[
            blk,
            pl.BlockSpec((CONV_WIDTH * SUBLANES, D_MODEL), lambda b, t: (0, 0)),
            pl.BlockSpec((1, D_MODEL), lambda b, t: (0, 0)),
        ],
        out_specs=blk,
        out_shape=jax.ShapeDtypeStruct(u.shape, F32),
        scratch_shapes=[pltpu.VMEM((SUBLANES, HIST_PAD + tt, D_MODEL), F32)],
        compiler_params=_cparams("arbitrary", "arbitrary"),
        name="conv_prompt",
    )(u, w_rep, b_dw)


def _conv_sample_kernel(full_ref, w_ref, b_ref, c_ref):
    bb = full_ref.shape[0]
    t_new = c_ref.shape[1]
    w = w_ref[...]
    bias = b_ref[...]
    for b in range(bb):
        for i in range(t_new):
            win = full_ref[b, i:i + CONV_WIDTH, :]
            c_ref[b, i:i + 1, :] = jnp.sum(win * w, axis=0, keepdims=True) + bias


def _conv_sample(full, w_dw, b_dw, bb):
    bsz, rows, _ = full.shape
    t_new = rows - (CONV_WIDTH - 1)
    return pl.pallas_call(
        _conv_sample_kernel,
        grid=(bsz // bb,),
        in_specs=[
            pl.BlockSpec((bb, rows, D_MODEL), lambda i: (i, 0, 0)),
            pl.BlockSpec((CONV_WIDTH, D_MODEL), lambda i: (0, 0)),
            pl.BlockSpec((1, D_MODEL), lambda i: (0, 0)),
        ],
        out_specs=pl.BlockSpec((bb, t_new, D_MODEL), lambda i: (i, 0, 0)),
        out_shape=jax.ShapeDtypeStruct((bsz, t_new, D_MODEL), F32),
        compiler_params=_cparams("arbitrary"),
        name="conv_sample",
    )(full, w_dw, b_dw)


def _proj_kernel(a_ref, w_ref, b_ref, res_ref, lng_ref, lnb_ref, gf_ref, y_ref, hn_ref, *, pre_ln):
    a = a_ref[...]
    if pre_ln:
        a = a.astype(F32)
        mu = jnp.mean(a, axis=-1, keepdims=True)
        ac = a - mu
        var = jnp.mean(ac * ac, axis=-1, keepdims=True)
        a = ac * lax.rsqrt(var + LN_EPS) * lng_ref[...] + lnb_ref[...]
        a = a * jax.nn.sigmoid(a)
    y = res_ref[...] + jnp.dot(a.astype(BF16), w_ref[...], preferred_element_type=F32) + b_ref[...]
    y_ref[...] = y
    hn_ref[...] = _rms(y, gf_ref[...])


def _proj(a, w_bf, b, res, ln_g, ln_b, g_ffn, tm, pre_ln):
    n = a.shape[0]
    row = pl.BlockSpec((tm, D_MODEL), lambda i: (i, 0))
    vec = pl.BlockSpec((1, D_MODEL), lambda i: (0, 0))
    return pl.pallas_call(
        functools.partial(_proj_kernel, pre_ln=pre_ln),
        grid=(n // tm,),
        in_specs=[row, pl.BlockSpec((D_MODEL, D_MODEL), lambda i: (0, 0)), vec, row, vec, vec, vec],
        out_specs=[row, row],
        out_shape=[jax.ShapeDtypeStruct((n, D_MODEL), F32), jax.ShapeDtypeStruct((n, D_MODEL), F32)],
        compiler_params=_cparams("arbitrary"),
        name="proj_ln" if pre_ln else "proj",
    )(a, w_bf, b, res, ln_g, ln_b, g_ffn)


def _router_kernel(hn_ref, w_ref, b_ref, idx_ref, gate_ref, cnt_ref):
    logits = jnp.dot(hn_ref[...], w_ref[...], preferred_element_type=F32,
                     precision=lax.Precision.HIGHEST) + b_ref[...]
    lane = lax.broadcasted_iota(jnp.int32, logits.shape, 1)
    neg = jnp.float32(-jnp.inf)
    cmask = (lane >= N_EXPERTS) & (lane < N_EXPERTS + N_GROUPS)
    cl = jnp.where(cmask, logits, neg)
    cmax = jnp.max(cl, axis=-1, keepdims=True)
    g_lane = jnp.min(jnp.where(cl == cmax, lane, LANES), axis=-1, keepdims=True)
    g_top = 1.0 / jnp.sum(jnp.exp(cl - cmax), axis=-1, keepdims=True)
    lo = (g_lane - N_EXPERTS) * EXPERTS_PER_GROUP
    fmask = (lane >= lo) & (lane < lo + EXPERTS_PER_GROUP)
    fl = jnp.where(fmask, logits, neg)
    m1 = jnp.max(fl, axis=-1, keepdims=True)
    i1 = jnp.min(jnp.where(fl == m1, lane, LANES), axis=-1, keepdims=True)
    fl2 = jnp.where(lane == i1, neg, fl)
    m2 = jnp.max(fl2, axis=-1, keepdims=True)
    i2 = jnp.min(jnp.where(fl2 == m2, lane, LANES), axis=-1, keepdims=True)
    r = jnp.exp(m2 - m1)
    w1 = 1.0 / (1.0 + r)
    w2 = r * w1
    tm = logits.shape[0]
    onehot = jnp.where((lane == i1) | (lane == i2), 1.0, 0.0)
    tr = lax.broadcasted_iota(jnp.int32, (tm, tm), 0)
    tc = lax.broadcasted_iota(jnp.int32, (tm, tm), 1)
    before = jnp.where(tc < tr, 1.0, 0.0).astype(BF16)

    @pl.when(pl.program_id(0) == 0)
    def _():
        cnt_ref[...] = jnp.zeros_like(cnt_ref)

    prior = jnp.dot(before, onehot.astype(BF16), preferred_element_type=F32) + cnt_ref[...]
    rank1 = jnp.sum(jnp.where(lane == i1, prior, 0.0), axis=-1, keepdims=True)
    rank2 = jnp.sum(jnp.where(lane == i2, prior, 0.0), axis=-1, keepdims=True)
    cnt_ref[...] += jnp.sum(onehot, axis=0, keepdims=True)
    idx = jnp.where(lane == 0, i1, jnp.where(lane == 1, i2, jnp.where(
        lane == 2, rank1.astype(jnp.int32), jnp.where(lane == 3, rank2.astype(jnp.int32), 0))))
    idx_ref[...] = idx
    gate_ref[...] = g_top * jnp.where(lane == 0, w1, jnp.where(lane == 1, w2, 0.0))


def _router(hn, w_r, b_r, tm):
    n = hn.shape[0]
    row = pl.BlockSpec((tm, LANES), lambda i: (i, 0))
    return pl.pallas_call(
        _router_kernel,
        grid=(n // tm,),
        in_specs=[
            pl.BlockSpec((tm, D_MODEL), lambda i: (i, 0)),
            pl.BlockSpec((D_MODEL, LANES), lambda i: (0, 0)),
            pl.BlockSpec((1, LANES), lambda i: (0, 0)),
        ],
        out_specs=[row, row, pl.BlockSpec((1, LANES), lambda i: (0, 0))],
        out_shape=[jax.ShapeDtypeStruct((n, LANES), jnp.int32), jax.ShapeDtypeStruct((n, LANES), F32),
                   jax.ShapeDtypeStruct((1, LANES), F32)],
        compiler_params=_cparams("arbitrary"),
        name="router",
    )(hn, w_r, b_r)


TOP_K = 2


def _row_copy_wait(src, dst, sem, rows):
    pltpu.make_async_copy(src.at[pl.ds(0, rows), :], dst.at[pl.ds(0, rows), :], sem).wait()


def _dispatch_kernel(pos_ref, x_ref, xs_init_ref, xs_ref, sem):
    del xs_init_ref
    tm = x_ref.shape[0]
    base = pl.program_id(0) * tm

    def issue(t, carry):
        for k in range(TOP_K):
            p = pos_ref[TOP_K * (base + t) + k]
            pltpu.make_async_copy(x_ref.at[pl.ds(t, 1), :], xs_ref.at[pl.ds(p, 1), :], sem).start()
        return carry

    lax.fori_loop(0, tm, issue, 0, unroll=4)
    for k in range(TOP_K):
        _row_copy_wait(x_ref, xs_ref, sem, tm)


def _dispatch(x, pos, n_rows, tm):
    n = x.shape[0]
    grid_spec = pltpu.PrefetchScalarGridSpec(
        num_scalar_prefetch=1,
        grid=(n // tm,),
        in_specs=[pl.BlockSpec((tm, D_MODEL), lambda i, pos: (i, 0)), pl.BlockSpec(memory_space=pl.ANY)],
        out_specs=pl.BlockSpec(memory_space=pl.ANY),
        scratch_shapes=[pltpu.SemaphoreType.DMA(())],
    )
    return pl.pallas_call(
        _dispatch_kernel,
        grid_spec=grid_spec,
        out_shape=jax.ShapeDtypeStruct((n_rows, D_MODEL), F32),
        input_output_aliases={2: 0},
        compiler_params=_cparams("arbitrary"),
        name="moe_dispatch",
    )(pos, x, jnp.zeros((n_rows, D_MODEL), F32))


def _moe_sorted_kernel(te_ref, xs_ref, wg_ref, wu_ref, wd_ref, ys_ref, wgb_ref, wub_ref, wdb_ref):
    i = pl.program_id(0)

    @pl.when((i == 0) | (te_ref[i] != te_ref[jnp.maximum(i - 1, 0)]))
    def _():
        wgb_ref[...] = wg_ref[0, 0].astype(BF16)
        wub_ref[...] = wu_ref[0, 0].astype(BF16)
        wdb_ref[...] = wd_ref[0, 0].astype(BF16)

    x = xs_ref[...].astype(BF16)
    hg = jnp.dot(x, wgb_ref[...], preferred_element_type=F32)
    hu = jnp.dot(x, wub_ref[...], preferred_element_type=F32)
    h = hg * jax.nn.sigmoid(hg) * hu
    ys_ref[...] = jnp.dot(h.astype(BF16), wdb_ref[...], preferred_element_type=F32)


def _moe_sorted(xs, tile_expert, w_gate, w_up, w_down, layer, tm):
    n_rows = xs.shape[0]
    row = pl.BlockSpec((tm, D_MODEL), lambda i, te: (i, 0))
    grid_spec = pltpu.PrefetchScalarGridSpec(
        num_scalar_prefetch=1,
        grid=(n_rows // tm,),
        in_specs=[
            row,
            pl.BlockSpec((1, 1, D_MODEL, D_EXPERT), lambda i, te: (layer, te[i], 0, 0)),
            pl.BlockSpec((1, 1, D_MODEL, D_EXPERT), lambda i, te: (layer, te[i], 0, 0)),
            pl.BlockSpec((1, 1, D_EXPERT, D_MODEL), lambda i, te: (layer, te[i], 0, 0)),
        ],
        out_specs=row,
        scratch_shapes=[pltpu.VMEM((D_MODEL, D_EXPERT), BF16), pltpu.VMEM((D_MODEL, D_EXPERT), BF16),
                        pltpu.VMEM((D_EXPERT, D_MODEL), BF16)],
    )
    return pl.pallas_call(
        _moe_sorted_kernel,
        grid_spec=grid_spec,
        out_shape=jax.ShapeDtypeStruct((n_rows, D_MODEL), F32),
        compiler_params=_cparams("arbitrary"),
        name="moe_experts",
    )(tile_expert, xs, w_gate, w_up, w_down)


def _combine_kernel(pos_ref, res_ref, gate_ref, gfin_ref, ys_ref, y_ref, buf_ref, sem, *, final_norm):
    tm = res_ref.shape[0]
    base = pl.program_id(0) * tm

    def issue(t, carry):
        for k in range(TOP_K):
            p = pos_ref[TOP_K * (base + t) + k]
            pltpu.make_async_copy(ys_ref.at[pl.ds(p, 1), :], buf_ref.at[k, pl.ds(t, 1), :], sem).start()
        return carry

    lax.fori_loop(0, tm, issue, 0, unroll=4)
    for k in range(TOP_K):
        _row_copy_wait(ys_ref, buf_ref.at[k], sem, tm)
    g = gate_ref[...]
    y = res_ref[...] + g[:, 0:1] * buf_ref[0] + g[:, 1:2] * buf_ref[1]
    if final_norm:
        y = _rms(y, gfin_ref[...])
    y_ref[...] = y


def _combine(ys, pos, res, gate, g_final, tm, final_norm):
    n = res.shape[0]
    row = pl.BlockSpec((tm, D_MODEL), lambda i, pos: (i, 0))
    grid_spec = pltpu.PrefetchScalarGridSpec(
        num_scalar_prefetch=1,
        grid=(n // tm,),
        in_specs=[row, pl.BlockSpec((tm, LANES), lambda i, pos: (i, 0)),
                  pl.BlockSpec((1, D_MODEL), lambda i, pos: (0, 0)), pl.BlockSpec(memory_space=pl.ANY)],
        out_specs=row,
        scratch_shapes=[pltpu.VMEM((TOP_K, tm, D_MODEL), F32), pltpu.SemaphoreType.DMA(())],
    )
    return pl.pallas_call(
        functools.partial(_combine_kernel, final_norm=final_norm),
        grid_spec=grid_spec,
        out_shape=jax.ShapeDtypeStruct((n, D_MODEL), F32),
        compiler_params=_cparams("arbitrary"),
        name="moe_combine_final" if final_norm else "moe_combine",
    )(pos, res, gate, g_final, ys)


def _route_plan(idx, cnt, tm_moe, n_tiles):
    experts = idx[:, 0:TOP_K]
    ranks = idx[:, TOP_K:2 * TOP_K]
    counts = cnt[0, :N_EXPERTS].astype(jnp.int32)
    padded = (counts + tm_moe - 1) // tm_moe * tm_moe
    ends = jnp.cumsum(padded)
    starts = ends - padded
    pos = (starts[experts] + ranks).reshape(-1)
    tile_start = jnp.arange(n_tiles, dtype=jnp.int32) * tm_moe
    tile_expert = jnp.sum((ends[None, :] <= tile_start[:, None]).astype(jnp.int32), axis=1)
    return pos, jnp.minimum(tile_expert, N_EXPERTS - 1)


ATT_BLK = 128
ATT_TQ = 512


NT_DIMS = (((1,), (1,)), ((), ()))


def _neg_suffix_ones(n):
    r = lax.broadcasted_iota(jnp.int32, (n, n), 0)
    c = lax.broadcasted_iota(jnp.int32, (n, n), 1)
    return jnp.where(r >= c, -1.0, 0.0).astype(BF16)


def _sb_weights(z2, neg_ge, run, mask):
    sp = _softplus2(z2)
    if mask is not None:
        sp = jnp.where(mask, sp, 0.0)
    c = jnp.dot(sp.astype(BF16), neg_ge, preferred_element_type=F32)
    a = jnp.exp2(z2 + c + run)
    if mask is not None:
        a = jnp.where(mask, a, 0.0)
    return a, run + c[:, 0:1]


def _attn_prompt_kernel(bias2_ref, q_ref, kt_ref, vt_ref, o_ref):
    hp = pl.program_id(1)
    qi = pl.program_id(2)
    nsub = ATT_TQ // ATT_BLK
    neg_ge = _neg_suffix_ones(ATT_BLK)
    lane = lax.broadcasted_iota(jnp.int32, (ATT_BLK, LANES), 1)
    row = lax.broadcasted_iota(jnp.int32, (ATT_BLK, ATT_BLK), 0)
    col = lax.broadcasted_iota(jnp.int32, (ATT_BLK, ATT_BLK), 1)
    diag_mask = col < row
    bias2 = [bias2_ref[2 * hp + h] for h in range(2)]

    chains = []
    for sq in range(nsub):
        q = q_ref[0, sq * ATT_BLK:(sq + 1) * ATT_BLK, :]
        for h in range(2):
            head = (lane >= h * HEAD_DIM) & (lane < (h + 1) * HEAD_DIM)
            chains.append((jnp.where(head, q, jnp.zeros_like(q)), bias2[h]))
    nch = len(chains)

    def chunk_pass(c0, nblks, runs, diagonal):
        kt_c = kt_ref[0, :, pl.ds(c0, ATT_TQ)]
        vt_c = vt_ref[0, :, pl.ds(c0, ATT_TQ)]
        blk = lambda x, b: x[:, b * ATT_BLK:(b + 1) * ATT_BLK]
        zs = [jnp.dot(qh, kt_c[:, :nb * ATT_BLK], preferred_element_type=F32) + bias
              for (qh, bias), nb in zip(chains, nblks)]
        runs = list(runs)
        parts = [[None] * nb for nb in nblks]
        for b in range(max(nblks) - 1, -1, -1):
            live = [n for n in range(nch) if nblks[n] > b]
            masked = {n: diagonal and b == nblks[n] - 1 for n in live}
            sps = {}
            for n in live:
                sp = _softplus2(blk(zs[n], b))
                if masked[n]:
                    sp = jnp.where(diag_mask, sp, 0.0)
                sps[n] = sp.astype(BF16)
            cs = {n: jnp.dot(sps[n], neg_ge, preferred_element_type=F32) for n in live}
            for n in live:
                a = jnp.exp2(blk(zs[n], b) + cs[n] + runs[n])
                if masked[n]:
                    a = jnp.where(diag_mask, a, 0.0)
                parts[n][b] = a.astype(BF16)
                runs[n] = runs[n] + cs[n][:, 0:1]
        outs = []
        for n in range(nch):
            a_c = parts[n][0] if nblks[n] == 1 else jnp.concatenate(parts[n], axis=1)
            outs.append(lax.dot_general(a_c, vt_c[:, :nblks[n] * ATT_BLK], NT_DIMS, preferred_element_type=F32))
        return outs, runs

    outs, runs = chunk_pass(pl.multiple_of(qi * ATT_TQ, ATT_TQ), [n // 2 + 1 for n in range(nch)],
                            [jnp.zeros((ATT_BLK, 1), F32)] * nch, True)
    state = [x for n in range(nch) for x in (outs[n], runs[n])]

    def body(it, st):
        outs, runs = chunk_pass(pl.multiple_of((qi - 1 - it) * ATT_TQ, ATT_TQ), [nsub] * nch,
                                [st[2 * n + 1] for n in range(nch)], False)
        return tuple(x for n in range(nch) for x in (st[2 * n] + outs[n], runs[n]))

    st = lax.fori_loop(0, qi, body, tuple(state))
    for sq in range(nsub):
        o_ref[0, sq * ATT_BLK:(sq + 1) * ATT_BLK, :] = jnp.where(
            lane < HEAD_DIM, st[4 * sq], st[4 * sq + 2]).astype(o_ref.dtype)


def _attn_prompt(q3, ktb, vtb, bias2):
    bsz, t_len, _ = q3.shape
    qblk = pl.BlockSpec((1, ATT_TQ, LANES), lambda b, hp, qi: (b, qi, hp))
    kvblk = pl.BlockSpec((1, LANES, t_len), lambda b, hp, qi: (b, hp, 0))
    o = pl.pallas_call(
        _attn_prompt_kernel,
        grid=(bsz, D_MODEL // LANES, t_len // ATT_TQ),
        in_specs=[pl.BlockSpec(memory_space=pltpu.SMEM), qblk, kvblk, kvblk],
        out_specs=qblk,
        out_shape=jax.ShapeDtypeStruct((bsz, t_len, D_MODEL), BF16),
        compiler_params=_cparams("arbitrary", "arbitrary", "arbitrary"),
        name="attn_prompt",
    )(bias2, q3, ktb, vtb)
    return o.reshape(bsz * t_len, D_MODEL)


SAMPLE_PAGES_PER_STEP = 8


def _attn_sample_kernel(pt_ref, q_ref, kn_ref, vn_ref, bias_ref, *refs):
    npg = SAMPLE_PAGES_PER_STEP
    k_refs = refs[:npg]
    v_refs = refs[npg:2 * npg]
    o_ref = refs[2 * npg]
    qbd_ref, acc_ref, carry_ref = refs[2 * npg + 1:]
    j = pl.program_id(1)
    t_new = q_ref.shape[1]
    rows = t_new * N_HEADS
    lane = lax.broadcasted_iota(jnp.int32, (N_HEADS, D_MODEL), 1)
    sub = lax.broadcasted_iota(jnp.int32, (N_HEADS, D_MODEL), 0)
    head_lanes = (lane >= sub * HEAD_DIM) & (lane < (sub + 1) * HEAD_DIM)
    bias = bias_ref[...]

    @pl.when(j == 0)
    def _():
        q = q_ref[0].astype(F32)
        qbd = jnp.concatenate(
            [jnp.where(head_lanes, jnp.broadcast_to(q[i:i + 1, :], (N_HEADS, D_MODEL)), 0.0) for i in range(t_new)],
            axis=0)
        qbd_ref[...] = qbd.astype(BF16)
        qi = lax.broadcasted_iota(jnp.int32, (rows, 1), 0) // N_HEADS
        kn = kn_ref[0]
        vn = vn_ref[0]
        carry = jnp.zeros((rows, 1), F32)
        acc = jnp.zeros((rows, D_MODEL), F32)
        for m in range(t_new - 1, -1, -1):
            valid = qi > m
            z2 = jnp.sum(qbd * kn[m:m + 1, :], axis=-1, keepdims=True) + bias
            sp = jnp.where(valid, _softplus2(z2), 0.0)
            a = jnp.where(valid, jnp.exp2(z2 - sp + carry), 0.0)
            acc = acc + a * vn[m:m + 1, :]
            carry = carry - sp
        acc_ref[...] = acc
        carry_ref[...] = carry

    neg_ge = _neg_suffix_ones(PAGE_SIZE)
    qbd = qbd_ref[...]
    acc = acc_ref[...]
    carry = carry_ref[...]
    zs = [jnp.dot(qbd, k_refs[r][0].astype(BF16), preferred_element_type=F32) + bias for r in range(npg)]
    cs = [jnp.dot(_softplus2(z2).astype(BF16), neg_ge, preferred_element_type=F32) for z2 in zs]
    weights = []
    for r in range(npg):
        weights.append(jnp.exp2(zs[r] + cs[r] + carry).astype(BF16))
        carry = carry + cs[r][:, 0:1]
    for r in range(npg):
        acc = acc + lax.dot_general(weights[r], v_refs[r][0].astype(BF16), NT_DIMS, preferred_element_type=F32)
    acc_ref[...] = acc
    carry_ref[...] = carry

    @pl.when(j == pl.num_programs(1) - 1)
    def _():
        for i in range(t_new):
            blk = jnp.where(head_lanes, acc[i * N_HEADS:(i + 1) * N_HEADS, :], 0.0)
            o_ref[0, i:i + 1, :] = jnp.sum(blk, axis=0, keepdims=True).astype(o_ref.dtype)


def _attn_sample(q, k_new, v_new, cache_kt, cache_vt, page_table, bias2):
    bsz, t_new, _ = q.shape
    n_pages = page_table.shape[1]
    npg = SAMPLE_PAGES_PER_STEP
    steps = n_pages // npg
    rows = t_new * N_HEADS
    bias_col = jnp.tile(bias2, t_new).reshape(rows, 1)
    tok = pl.BlockSpec((1, t_new, D_MODEL), lambda b, j, pt: (b, 0, 0))

    def page_spec(r):
        return pl.BlockSpec((1, D_MODEL, PAGE_SIZE), lambda b, j, pt: (pt[b, n_pages - 1 - j * npg - r], 0, 0))

    grid_spec = pltpu.PrefetchScalarGridSpec(
        num_scalar_prefetch=1,
        grid=(bsz, steps),
        in_specs=[tok, tok, tok, pl.BlockSpec((rows, 1), lambda b, j, pt: (0, 0))]
        + [page_spec(r) for r in range(npg)] + [page_spec(r) for r in range(npg)],
        out_specs=tok,
        scratch_shapes=[
            pltpu.VMEM((rows, D_MODEL), BF16),
            pltpu.VMEM((rows, D_MODEL), F32),
            pltpu.VMEM((rows, 1), F32),
        ],
    )
    return pl.pallas_call(
        _attn_sample_kernel,
        grid_spec=grid_spec,
        out_shape=jax.ShapeDtypeStruct((bsz, t_new, D_MODEL), BF16),
        compiler_params=_cparams("arbitrary", "arbitrary"),
        name="attn_sample",
    )(page_table, q, k_new, v_new, bias_col, *([cache_kt] * npg), *([cache_vt] * npg))


def _router_params(w_coarse, b_coarse, w_fine, b_fine):
    wf = jnp.transpose(w_fine, (1, 0, 2)).reshape(D_MODEL, N_EXPERTS)
    w = jnp.concatenate([wf, w_coarse, jnp.zeros((D_MODEL, LANES - N_EXPERTS - N_GROUPS), F32)], axis=1)
    b = jnp.concatenate([b_fine.reshape(-1), b_coarse, jnp.zeros((LANES - N_EXPERTS - N_GROUPS,), F32)])
    return w, b.reshape(1, LANES)


def _ffn(hn, res, mw, g_final, tm, tm_moe, final_norm):
    w_r, b_r, wg, wu, wd, layer = mw
    n = hn.shape[0]
    idx, gate, cnt = _router(hn, w_r, b_r, tm)
    n_tiles = pl.cdiv(TOP_K * n, tm_moe) + N_EXPERTS
    pos, tile_expert = _route_plan(idx, cnt, tm_moe, n_tiles)
    xs = _dispatch(hn, pos, n_tiles * tm_moe, tm)
    ys = _moe_sorted(xs, tile_expert, wg, wu, wd, layer, tm_moe)
    return _combine(ys, pos, res, gate, g_final, tm, final_norm)


def kernel(x_prompt, x_sample, state_conv, cache_k, cache_v, page_table, norm_mix, norm_ffn, norm_final,
           conv_w_pw1, conv_b_pw1, conv_w_dw, conv_b_dw, conv_ln_g, conv_ln_b, conv_w_pw2, conv_b_pw2,
           sb_w_qkv, sb_w_o, sb_logit_bias, moe_w_coarse, moe_b_coarse, moe_w_fine, moe_b_fine,
           moe_w_gate, moe_w_up, moe_w_down):
    bsz, t_len, _ = x_prompt.shape
    dbsz, t_new, _ = x_sample.shape
    n_p = bsz * t_len
    n_s = dbsz * t_new
    tm_p, tm_s = 512, n_s
    tmm_p, tmm_s = 256, 128
    vec = lambda a: a.reshape(1, -1)
    g_fin = vec(norm_final)

    xp = x_prompt.reshape(n_p, D_MODEL)
    xs = x_sample.reshape(n_s, D_MODEL)

    moe_w = []
    for i in range(2):
        w_r, b_r = _router_params(moe_w_coarse[i], moe_b_coarse[i], moe_w_fine[i], moe_b_fine[i])
        moe_w.append((w_r, b_r, moe_w_gate, moe_w_up, moe_w_down, i))

    w1 = conv_w_pw1[0].astype(BF16)
    w2 = conv_w_pw2[0].astype(BF16)
    up = _pw1_glu(xp, vec(norm_mix[0]), w1, vec(conv_b_pw1[0]), tm_p)
    us = _pw1_glu(xs, vec(norm_mix[0]), w1, vec(conv_b_pw1[0]), tm_s)
    up3 = up.reshape(bsz, t_len, D_MODEL)
    full_s = jnp.concatenate([state_conv[0], us.reshape(dbsz, t_new, D_MODEL)], axis=1)
    conv_state_p = up3[:, t_len - (CONV_WIDTH - 1):, :][None]
    conv_state_s = full_s[:, t_new:, :][None]
    cp = _conv_prompt(up3, conv_w_dw[0], vec(conv_b_dw[0]), 256).reshape(n_p, D_MODEL)
    cs = _conv_sample(full_s, conv_w_dw[0], vec(conv_b_dw[0]), 8).reshape(n_s, D_MODEL)
    ln = (vec(conv_ln_g[0]), vec(conv_ln_b[0]), vec(norm_ffn[0]))
    yp, hp = _proj(cp, w2, vec(conv_b_pw2[0]), xp, *ln, tm_p, True)
    ys, hs = _proj(cs, w2, vec(conv_b_pw2[0]), xs, *ln, tm_s, True)
    yp = _ffn(hp, yp, moe_w[0], g_fin, tm_p, tmm_p, False)
    ys = _ffn(hs, ys, moe_w[0], g_fin, tm_s, tmm_s, False)

    wqkv = sb_w_qkv[0].astype(BF16)
    wo = sb_w_o[0].astype(BF16)
    bias2 = sb_logit_bias[0].astype(F32) * LOG2E
    qp, ktp, vtp, ktb, vtb = _qkv_t(yp.reshape(bsz, t_len, D_MODEL), vec(norm_mix[1]), wqkv[:, :D_MODEL],
                                    wqkv[:, D_MODEL:2 * D_MODEL].T, wqkv[:, 2 * D_MODEL:].T, tm_p)
    qs, ks, vs = _qkv(ys, vec(norm_mix[1]), wqkv, tm_s)
    op = _attn_prompt(qp, ktb, vtb, bias2)
    n_pool = cache_k.shape[1]
    cache_kt = jnp.transpose(cache_k[0], (0, 2, 3, 1)).reshape(n_pool, D_MODEL, PAGE_SIZE)
    cache_vt = jnp.transpose(cache_v[0], (0, 2, 3, 1)).reshape(n_pool, D_MODEL, PAGE_SIZE)
    os_ = _attn_sample(qs.reshape(dbsz, t_new, D_MODEL), ks.reshape(dbsz, t_new, D_MODEL),
                       vs.reshape(dbsz, t_new, D_MODEL), cache_kt, cache_vt,
                       page_table, bias2).reshape(n_s, D_MODEL)
    zero_b = jnp.zeros((1, D_MODEL), F32)
    nf1 = vec(norm_ffn[1])
    yp, hp = _proj(op, wo, zero_b, yp, nf1, nf1, nf1, tm_p, False)
    ys, hs = _proj(os_, wo, zero_b, ys, nf1, nf1, nf1, tm_s, False)
    yp = _ffn(hp, yp, moe_w[1], g_fin, tm_p, tmm_p, True)
    ys = _ffn(hs, ys, moe_w[1], g_fin, tm_s, tmm_s, True)

    def rows_p(xt):
        return jnp.transpose(xt.reshape(bsz, N_HEADS, HEAD_DIM, t_len), (0, 3, 1, 2))[None]

    hshape_s = (1, dbsz, t_new, N_HEADS, HEAD_DIM)
    return (yp.reshape(bsz, t_len, D_MODEL), ys.reshape(dbsz, t_new, D_MODEL), conv_state_p, conv_state_s,
            rows_p(ktp), rows_p(vtp), ks.reshape(hshape_s), vs.reshape(hshape_s))
```

```python
import functools

import jax
import jax.numpy as jnp
from jax import lax
from jax.experimental import pallas as pl
from jax.experimental.pallas import tpu as pltpu

D_MODEL = 1024
CONV_WIDTH = 31
N_HEADS = 16
HEAD_DIM = D_MODEL // N_HEADS
N_GROUPS = 4
EXPERTS_PER_GROUP = 8
N_EXPERTS = N_GROUPS * EXPERTS_PER_GROUP
D_EXPERT = D_MODEL // 4
PAGE_SIZE = 128
NORM_EPS = 1e-6
LN_EPS = 1e-5

LANES = 128
SUBLANES = 8
HIST_PAD = 32
VMEM_LIMIT = 56 * 1024 * 1024

F32 = jnp.float32
BF16 = jnp.bfloat16


def _cparams(*sem):
    return pltpu.CompilerParams(dimension_semantics=sem, vmem_limit_bytes=VMEM_LIMIT)


def _rms(x, g):
    return x * lax.rsqrt(jnp.mean(x * x, axis=-1, keepdims=True) + NORM_EPS) * g


LOG2E = 1.4426950408889634
POW2_CLAMP = 64.0


def _softplus2(z2):
    return jnp.where(z2 > POW2_CLAMP, z2, jnp.log2(1.0 + jnp.exp2(jnp.minimum(z2, POW2_CLAMP))))


def _pw1_glu_kernel(x_ref, g_ref, w_ref, b_ref, u_ref):
    h = _rms(x_ref[...], g_ref[...]).astype(BF16)
    a = jnp.dot(h, w_ref[...], preferred_element_type=F32) + b_ref[...]
    u_ref[...] = a[:, :D_MODEL] * jax.nn.sigmoid(a[:, D_MODEL:])


def _pw1_glu(x, g, w_bf, b, tm):
    n = x.shape[0]
    return pl.pallas_call(
        _pw1_glu_kernel,
        grid=(n // tm,),
        in_specs=[
            pl.BlockSpec((tm, D_MODEL), lambda i: (i, 0)),
            pl.BlockSpec((1, D_MODEL), lambda i: (0, 0)),
            pl.BlockSpec((D_MODEL, 2 * D_MODEL), lambda i: (0, 0)),
            pl.BlockSpec((1, 2 * D_MODEL), lambda i: (0, 0)),
        ],
        out_specs=pl.BlockSpec((tm, D_MODEL), lambda i: (i, 0)),
        out_shape=jax.ShapeDtypeStruct((n, D_MODEL), F32),
        compiler_params=_cparams("arbitrary"),
        name="pw1_glu",
    )(x, g, w_bf, b)


Q_SCALE2 = (HEAD_DIM ** -0.5) * LOG2E


def _qkv_kernel(x_ref, g_ref, w_ref, q_ref, k_ref, v_ref):
    h = _rms(x_ref[...], g_ref[...]).astype(BF16)
    a = jnp.dot(h, w_ref[...], preferred_element_type=F32)
    q_ref[...] = (a[:, :D_MODEL] * Q_SCALE2).astype(BF16)
    k_ref[...] = a[:, D_MODEL:2 * D_MODEL]
    v_ref[...] = a[:, 2 * D_MODEL:]


def _qkv(x, g, w_bf, tm):
    n = x.shape[0]
    row = pl.BlockSpec((tm, D_MODEL), lambda i: (i, 0))
    return pl.pallas_call(
        _qkv_kernel,
        grid=(n // tm,),
        in_specs=[
            row,
            pl.BlockSpec((1, D_MODEL), lambda i: (0, 0)),
            pl.BlockSpec((D_MODEL, 3 * D_MODEL), lambda i: (0, 0)),
        ],
        out_specs=[row, row, row],
        out_shape=[
            jax.ShapeDtypeStruct((n, D_MODEL), BF16),
            jax.ShapeDtypeStruct((n, D_MODEL), F32),
            jax.ShapeDtypeStruct((n, D_MODEL), F32),
        ],
        compiler_params=_cparams("arbitrary"),
        name="qkv",
    )(x, g, w_bf)


def _qkv_t_kernel(x_ref, g_ref, wq_ref, wkt_ref, wvt_ref, q_ref, kt_ref, vt_ref, ktb_ref, vtb_ref):
    h = _rms(x_ref[0], g_ref[...]).astype(BF16)
    q_ref[0] = (jnp.dot(h, wq_ref[...], preferred_element_type=F32) * Q_SCALE2).astype(BF16)
    nt = (((1,), (1,)), ((), ()))
    kt = lax.dot_general(wkt_ref[...], h, nt, preferred_element_type=F32)
    vt = lax.dot_general(wvt_ref[...], h, nt, preferred_element_type=F32)
    kt_ref[0] = kt
    vt_ref[0] = vt
    ktb_ref[0] = kt.astype(BF16)
    vtb_ref[0] = vt.astype(BF16)


def _qkv_t(x3, g, wq_bf, wkt_bf, wvt_bf, tm):
    bsz, t_len, _ = x3.shape
    row = pl.BlockSpec((1, tm, D_MODEL), lambda b, t: (b, t, 0))
    col = pl.BlockSpec((1, D_MODEL, tm), lambda b, t: (b, 0, t))
    wspec = pl.BlockSpec((D_MODEL, D_MODEL), lambda b, t: (0, 0))
    tshape = (bsz, D_MODEL, t_len)
    return pl.pallas_call(
        _qkv_t_kernel,
        grid=(bsz, t_len // tm),
        in_specs=[row, pl.BlockSpec((1, D_MODEL), lambda b, t: (0, 0)), wspec, wspec, wspec],
        out_specs=[row, col, col, col, col],
        out_shape=[
            jax.ShapeDtypeStruct((bsz, t_len, D_MODEL), BF16),
            jax.ShapeDtypeStruct(tshape, F32),
            jax.ShapeDtypeStruct(tshape, F32),
            jax.ShapeDtypeStruct(tshape, BF16),
            jax.ShapeDtypeStruct(tshape, BF16),
        ],
        compiler_params=_cparams("arbitrary", "arbitrary"),
        name="qkv_t",
    )(x3, g, wq_bf, wkt_bf, wvt_bf)


CONV_ROWS = 32


def _conv_prompt_kernel(u_ref, w_ref, b_ref, c_ref, sh_ref):
    tt = u_ref.shape[1]
    rows = HIST_PAD + tt
    t = pl.program_id(1)

    @pl.when(t == 0)
    def _():
        sh_ref[0, 0:HIST_PAD, :] = jnp.zeros((HIST_PAD, D_MODEL), F32)

    sh_ref[0, HIST_PAD:rows, :] = u_ref[0]
    for s in range(1, SUBLANES):
        sh_ref[s] = pltpu.roll(sh_ref[0], rows - s, axis=0)
    first = HIST_PAD - (CONV_WIDTH - 1)
    bias = b_ref[...]

    def body(r, carry):
        r0 = pl.multiple_of(r * CONV_ROWS, CONV_ROWS)
        accs = [jnp.zeros((SUBLANES, D_MODEL), F32) + bias for _ in range(CONV_ROWS // SUBLANES)]
        for j in range(CONV_WIDTH):
            s, a = (first + j) % SUBLANES, (first + j) // SUBLANES
            w = w_ref[j * SUBLANES:(j + 1) * SUBLANES, :]
            for i in range(len(accs)):
                accs[i] = accs[i] + sh_ref[s, pl.ds(r0 + (a + i) * SUBLANES, SUBLANES), :] * w
        for i, acc in enumerate(accs):
            c_ref[0, pl.ds(r0 + i * SUBLANES, SUBLANES), :] = acc
        return carry

    lax.fori_loop(0, tt // CONV_ROWS, body, 0)
    sh_ref[0, 0:HIST_PAD, :] = sh_ref[0, tt:rows, :]


def _conv_prompt(u, w_dw, b_dw, tt):
    bsz, t_len, _ = u.shape
    blk = pl.BlockSpec((1, tt, D_MODEL), lambda b, t: (b, t, 0))
    w_rep = jnp.repeat(w_dw, SUBLANES, axis=0)
    return pl.pallas_call(
        _conv_prompt_kernel,
        grid=(bsz, t_len // tt),
        in_specs=[
            blk,
            pl.BlockSpec((CONV_WIDTH * SUBLANES, D_MODEL), lambda b, t: (0, 0)),
            pl.BlockSpec((1, D_MODEL), lambda b, t: (0, 0)),
        ],
        out_specs=blk,
        out_shape=jax.ShapeDtypeStruct(u.shape, F32),
        scratch_shapes=[pltpu.VMEM((SUBLANES, HIST_PAD + tt, D_MODEL), F32)],
        compiler_params=_cparams("arbitrary", "arbitrary"),
        name="conv_prompt",
    )(u, w_rep, b_dw)


def _conv_sample_kernel(full_ref, w_ref, b_ref, c_ref):
    bb = full_ref.shape[0]
    t_new = c_ref.shape[1]
    w = w_ref[...]
    bias = b_ref[...]
    for b in range(bb):
        for i in range(t_new):
            win = full_ref[b, i:i + CONV_WIDTH, :]
            c_ref[b, i:i + 1, :] = jnp.sum(win * w, axis=0, keepdims=True) + bias


def _conv_sample(full, w_dw, b_dw, bb):
    bsz, rows, _ = full.shape
    t_new = rows - (CONV_WIDTH - 1)
    return pl.pallas_call(
        _conv_sample_kernel,
        grid=(bsz // bb,),
        in_specs=[
            pl.BlockSpec((bb, rows, D_MODEL), lambda i: (i, 0, 0)),
            pl.BlockSpec((CONV_WIDTH, D_MODEL), lambda i: (0, 0)),
            pl.BlockSpec((1, D_MODEL), lambda i: (0, 0)),
        ],
        out_specs=pl.BlockSpec((bb, t_new, D_MODEL), lambda i: (i, 0, 0)),
        out_shape=jax.ShapeDtypeStruct((bsz, t_new, D_MODEL), F32),
        compiler_params=_cparams("arbitrary"),
        name="conv_sample",
    )(full, w_dw, b_dw)


def _proj_kernel(a_ref, w_ref, b_ref, res_ref, lng_ref, lnb_ref, gf_ref, y_ref, hn_ref, *, pre_ln):
    a = a_ref[...]
    if pre_ln:
        a = a.astype(F32)
        mu = jnp.mean(a, axis=-1, keepdims=True)
        ac = a - mu
        var = jnp.mean(ac * ac, axis=-1, keepdims=True)
        a = ac * lax.rsqrt(var + LN_EPS) * lng_ref[...] + lnb_ref[...]
        a = a * jax.nn.sigmoid(a)
    y = res_ref[...] + jnp.dot(a.astype(BF16), w_ref[...], preferred_element_type=F32) + b_ref[...]
    y_ref[...] = y
    hn_ref[...] = _rms(y, gf_ref[...])


def _proj(a, w_bf, b, res, ln_g, ln_b, g_ffn, tm, pre_ln):
    n = a.shape[0]
    row = pl.BlockSpec((tm, D_MODEL), lambda i: (i, 0))
    vec = pl.BlockSpec((1, D_MODEL), lambda i: (0, 0))
    return pl.pallas_call(
        functools.partial(_proj_kernel, pre_ln=pre_ln),
        grid=(n // tm,),
        in_specs=[row, pl.BlockSpec((D_MODEL, D_MODEL), lambda i: (0, 0)), vec, row, vec, vec, vec],
        out_specs=[row, row],
        out_shape=[jax.ShapeDtypeStruct((n, D_MODEL), F32), jax.ShapeDtypeStruct((n, D_MODEL), F32)],
        compiler_params=_cparams("arbitrary"),
        name="proj_ln" if pre_ln else "proj",
    )(a, w_bf, b, res, ln_g, ln_b, g_ffn)


def _router_kernel(hn_ref, w_ref, b_ref, idx_ref, gate_ref, cnt_ref):
    logits = jnp.dot(hn_ref[...], w_ref[...], preferred_element_type=F32,
                     precision=lax.Precision.HIGHEST) + b_ref[...]
    lane = lax.broadcasted_iota(jnp.int32, logits.shape, 1)
    neg = jnp.float32(-jnp.inf)
    cmask = (lane >= N_EXPERTS) & (lane < N_EXPERTS + N_GROUPS)
    cl = jnp.where(cmask, logits, neg)
    cmax = jnp.max(cl, axis=-1, keepdims=True)
    g_lane = jnp.min(jnp.where(cl == cmax, lane, LANES), axis=-1, keepdims=True)
    g_top = 1.0 / jnp.sum(jnp.exp(cl - cmax), axis=-1, keepdims=True)
    lo = (g_lane - N_EXPERTS) * EXPERTS_PER_GROUP
    fmask = (lane >= lo) & (lane < lo + EXPERTS_PER_GROUP)
    fl = jnp.where(fmask, logits, neg)
    m1 = jnp.max(fl, axis=-1, keepdims=True)
    i1 = jnp.min(jnp.where(fl == m1, lane, LANES), axis=-1, keepdims=True)
    fl2 = jnp.where(lane == i1, neg, fl)
    m2 = jnp.max(fl2, axis=-1, keepdims=True)
    i2 = jnp.min(jnp.where(fl2 == m2, lane, LANES), axis=-1, keepdims=True)
    r = jnp.exp(m2 - m1)
    w1 = 1.0 / (1.0 + r)
    w2 = r * w1
    tm = logits.shape[0]
    onehot = jnp.where((lane == i1) | (lane == i2), 1.0, 0.0)
    tr = lax.broadcasted_iota(jnp.int32, (tm, tm), 0)
    tc = lax.broadcasted_iota(jnp.int32, (tm, tm), 1)
    before = jnp.where(tc < tr, 1.0, 0.0).astype(BF16)

    @pl.when(pl.program_id(0) == 0)
    def _():
        cnt_ref[...] = jnp.zeros_like(cnt_ref)

    prior = jnp.dot(before, onehot.astype(BF16), preferred_element_type=F32) + cnt_ref[...]
    rank1 = jnp.sum(jnp.where(lane == i1, prior, 0.0), axis=-1, keepdims=True)
    rank2 = jnp.sum(jnp.where(lane == i2, prior, 0.0), axis=-1, keepdims=True)
    cnt_ref[...] += jnp.sum(onehot, axis=0, keepdims=True)
    idx = jnp.where(lane == 0, i1, jnp.where(lane == 1, i2, jnp.where(
        lane == 2, rank1.astype(jnp.int32), jnp.where(lane == 3, rank2.astype(jnp.int32), 0))))
    idx_ref[...] = idx
    gate_ref[...] = g_top * jnp.where(lane == 0, w1, jnp.where(lane == 1, w2, 0.0))


def _router(hn, w_r, b_r, tm):
    n = hn.shape[0]
    row = pl.BlockSpec((tm, LANES), lambda i: (i, 0))
    return pl.pallas_call(
        _router_kernel,
        grid=(n // tm,),
        in_specs=[
            pl.BlockSpec((tm, D_MODEL), lambda i: (i, 0)),
            pl.BlockSpec((D_MODEL, LANES), lambda i: (0, 0)),
            pl.BlockSpec((1, LANES), lambda i: (0, 0)),
        ],
        out_specs=[row, row, pl.BlockSpec((1, LANES), lambda i: (0, 0))],
        out_shape=[jax.ShapeDtypeStruct((n, LANES), jnp.int32), jax.ShapeDtypeStruct((n, LANES), F32),
                   jax.ShapeDtypeStruct((1, LANES), F32)],
        compiler_params=_cparams("arbitrary"),
        name="router",
    )(hn, w_r, b_r)


TOP_K = 2


def _row_copy_wait(src, dst, sem, rows):
    pltpu.make_async_copy(src.at[pl.ds(0, rows), :], dst.at[pl.ds(0, rows), :], sem).wait()


def _dispatch_kernel(pos_ref, x_ref, xs_init_ref, xs_ref, sem):
    del xs_init_ref
    tm = x_ref.shape[0]
    base = pl.program_id(0) * tm

    def issue(t, carry):
        for k in range(TOP_K):
            p = pos_ref[TOP_K * (base + t) + k]
            pltpu.make_async_copy(x_ref.at[pl.ds(t, 1), :], xs_ref.at[pl.ds(p, 1), :], sem).start()
        return carry

    lax.fori_loop(0, tm, issue, 0, unroll=4)
    for k in range(TOP_K):
        _row_copy_wait(x_ref, xs_ref, sem, tm)


def _dispatch(x, pos, xs_init, tm):
    n = x.shape[0]
    n_rows = xs_init.shape[0]
    grid_spec = pltpu.PrefetchScalarGridSpec(
        num_scalar_prefetch=1,
        grid=(n // tm,),
        in_specs=[pl.BlockSpec((tm, D_MODEL), lambda i, pos: (i, 0)), pl.BlockSpec(memory_space=pl.ANY)],
        out_specs=pl.BlockSpec(memory_space=pl.ANY),
        scratch_shapes=[pltpu.SemaphoreType.DMA(())],
    )
    return pl.pallas_call(
        _dispatch_kernel,
        grid_spec=grid_spec,
        out_shape=jax.ShapeDtypeStruct((n_rows, D_MODEL), F32),
        input_output_aliases={2: 0},
        compiler_params=_cparams("arbitrary"),
        name="moe_dispatch",
    )(pos, x, xs_init)


def _moe_sorted_kernel(te_ref, xs_ref, wg_ref, wu_ref, wd_ref, ys_ref, wgb_ref, wub_ref, wdb_ref):
    i = pl.program_id(0)

    @pl.when((i == 0) | (te_ref[i] != te_ref[jnp.maximum(i - 1, 0)]))
    def _():
        wgb_ref[...] = wg_ref[0, 0].astype(BF16)
        wub_ref[...] = wu_ref[0, 0].astype(BF16)
        wdb_ref[...] = wd_ref[0, 0].astype(BF16)

    x = xs_ref[...].astype(BF16)
    hg = jnp.dot(x, wgb_ref[...], preferred_element_type=F32)
    hu = jnp.dot(x, wub_ref[...], preferred_element_type=F32)
    h = hg * jax.nn.sigmoid(hg) * hu
    ys_ref[...] = jnp.dot(h.astype(BF16), wdb_ref[...], preferred_element_type=F32)


def _moe_sorted(xs, tile_expert, w_gate, w_up, w_down, layer, tm):
    n_rows = xs.shape[0]
    row = pl.BlockSpec((tm, D_MODEL), lambda i, te: (i, 0))
    grid_spec = pltpu.PrefetchScalarGridSpec(
        num_scalar_prefetch=1,
        grid=(n_rows // tm,),
        in_specs=[
            row,
            pl.BlockSpec((1, 1, D_MODEL, D_EXPERT), lambda i, te: (layer, te[i], 0, 0)),
            pl.BlockSpec((1, 1, D_MODEL, D_EXPERT), lambda i, te: (layer, te[i], 0, 0)),
            pl.BlockSpec((1, 1, D_EXPERT, D_MODEL), lambda i, te: (layer, te[i], 0, 0)),
        ],
        out_specs=row,
        scratch_shapes=[pltpu.VMEM((D_MODEL, D_EXPERT), BF16), pltpu.VMEM((D_MODEL, D_EXPERT), BF16),
                        pltpu.VMEM((D_EXPERT, D_MODEL), BF16)],
    )
    return pl.pallas_call(
        _moe_sorted_kernel,
        grid_spec=grid_spec,
        out_shape=jax.ShapeDtypeStruct((n_rows, D_MODEL), F32),
        compiler_params=_cparams("arbitrary"),
        name="moe_experts",
    )(tile_expert, xs, w_gate, w_up, w_down)


def _combine_kernel(pos_ref, res_ref, gate_ref, gfin_ref, ys_ref, y_ref, buf_ref, sem, *, final_norm):
    tm = res_ref.shape[0]
    base = pl.program_id(0) * tm

    def issue(t, carry):
        for k in range(TOP_K):
            p = pos_ref[TOP_K * (base + t) + k]
            pltpu.make_async_copy(ys_ref.at[pl.ds(p, 1), :], buf_ref.at[k, pl.ds(t, 1), :], sem).start()
        return carry

    lax.fori_loop(0, tm, issue, 0, unroll=4)
    for k in range(TOP_K):
        _row_copy_wait(ys_ref, buf_ref.at[k], sem, tm)
    g = gate_ref[...]
    y = res_ref[...] + g[:, 0:1] * buf_ref[0] + g[:, 1:2] * buf_ref[1]
    if final_norm:
        y = _rms(y, gfin_ref[...])
    y_ref[...] = y


def _combine(ys, pos, res, gate, g_final, tm, final_norm):
    n = res.shape[0]
    row = pl.BlockSpec((tm, D_MODEL), lambda i, pos: (i, 0))
    grid_spec = pltpu.PrefetchScalarGridSpec(
        num_scalar_prefetch=1,
        grid=(n // tm,),
        in_specs=[row, pl.BlockSpec((tm, LANES), lambda i, pos: (i, 0)),
                  pl.BlockSpec((1, D_MODEL), lambda i, pos: (0, 0)), pl.BlockSpec(memory_space=pl.ANY)],
        out_specs=row,
        scratch_shapes=[pltpu.VMEM((TOP_K, tm, D_MODEL), F32), pltpu.SemaphoreType.DMA(())],
    )
    return pl.pallas_call(
        functools.partial(_combine_kernel, final_norm=final_norm),
        grid_spec=grid_spec,
        out_shape=jax.ShapeDtypeStruct((n, D_MODEL), F32),
        compiler_params=_cparams("arbitrary"),
        name="moe_combine_final" if final_norm else "moe_combine",
    )(pos, res, gate, g_final, ys)


def _pos_kernel(idx_ref, starts_ref, pos_ref):
    idx = idx_ref[...]
    lane = lax.broadcasted_iota(jnp.int32, idx.shape, 1)
    starts = starts_ref[...]
    pos = []
    for k in range(TOP_K):
        first = jnp.sum(jnp.where(lane == idx[:, k:k + 1], starts, 0.0), axis=-1, keepdims=True)
        pos.append(first.astype(jnp.int32) + idx[:, TOP_K + k:TOP_K + k + 1])
    pos_ref[...] = jnp.where(lane == 0, pos[0], jnp.where(lane == 1, pos[1], 0))


def _route_plan(idx, cnt, tm, tm_moe, n_tiles):
    n = idx.shape[0]
    counts = cnt[0].astype(jnp.int32)
    padded = (counts + tm_moe - 1) // tm_moe * tm_moe
    ends = jnp.cumsum(padded)
    starts = (ends - padded).astype(F32).reshape(1, LANES)
    row = pl.BlockSpec((tm, LANES), lambda i: (i, 0))
    pos = pl.pallas_call(
        _pos_kernel,
        grid=(n // tm,),
        in_specs=[row, pl.BlockSpec((1, LANES), lambda i: (0, 0))],
        out_specs=row,
        out_shape=jax.ShapeDtypeStruct((n, LANES), jnp.int32),
        compiler_params=_cparams("arbitrary"),
        name="moe_pos",
    )(idx, starts)[:, :TOP_K].reshape(-1)
    tile_start = jnp.arange(n_tiles, dtype=jnp.int32) * tm_moe
    tile_expert = jnp.sum((ends[None, :N_EXPERTS] <= tile_start[:, None]).astype(jnp.int32), axis=1)
    return pos, jnp.minimum(tile_expert, N_EXPERTS - 1)


ATT_BLK = 128
ATT_TQ = 512


NT_DIMS = (((1,), (1,)), ((), ()))


def _neg_suffix_ones(n):
    r = lax.broadcasted_iota(jnp.int32, (n, n), 0)
    c = lax.broadcasted_iota(jnp.int32, (n, n), 1)
    return jnp.where(r >= c, -1.0, 0.0).astype(BF16)


def _sb_weights(z2, neg_ge, run, mask):
    sp = _softplus2(z2)
    if mask is not None:
        sp = jnp.where(mask, sp, 0.0)
    c = jnp.dot(sp.astype(BF16), neg_ge, preferred_element_type=F32)
    a = jnp.exp2(z2 + c + run)
    if mask is not None:
        a = jnp.where(mask, a, 0.0)
    return a, run + c[:, 0:1]


def _attn_prompt_kernel(bias2_ref, q_ref, kt_ref, vt_ref, o_ref):
    hp = pl.program_id(1)
    qi = pl.program_id(2)
    nsub = ATT_TQ // ATT_BLK
    neg_ge = _neg_suffix_ones(ATT_BLK)
    lane = lax.broadcasted_iota(jnp.int32, (ATT_BLK, LANES), 1)
    row = lax.broadcasted_iota(jnp.int32, (ATT_BLK, ATT_BLK), 0)
    col = lax.broadcasted_iota(jnp.int32, (ATT_BLK, ATT_BLK), 1)
    diag_mask = col < row
    bias2 = [bias2_ref[2 * hp + h] for h in range(2)]

    chains = []
    for sq in range(nsub):
        q = q_ref[0, sq * ATT_BLK:(sq + 1) * ATT_BLK, :]
        for h in range(2):
            head = (lane >= h * HEAD_DIM) & (lane < (h + 1) * HEAD_DIM)
            chains.append((jnp.where(head, q, jnp.zeros_like(q)), bias2[h]))
    nch = len(chains)

    def chunk_pass(c0, nblks, runs, diagonal):
        kt_c = kt_ref[0, :, pl.ds(c0, ATT_TQ)]
        vt_c = vt_ref[0, :, pl.ds(c0, ATT_TQ)]
        blk = lambda x, b: x[:, b * ATT_BLK:(b + 1) * ATT_BLK]
        zs = [jnp.dot(qh, kt_c[:, :nb * ATT_BLK], preferred_element_type=F32) + bias
              for (qh, bias), nb in zip(chains, nblks)]
        runs = list(runs)
        parts = [[None] * nb for nb in nblks]
        for b in range(max(nblks) - 1, -1, -1):
            live = [n for n in range(nch) if nblks[n] > b]
            masked = {n: diagonal and b == nblks[n] - 1 for n in live}
            sps = {}
            for n in live:
                sp = _softplus2(blk(zs[n], b))
                if masked[n]:
                    sp = jnp.where(diag_mask, sp, 0.0)
                sps[n] = sp.astype(BF16)
            cs = {n: jnp.dot(sps[n], neg_ge, preferred_element_type=F32) for n in live}
            for n in live:
                a = jnp.exp2(blk(zs[n], b) + cs[n] + runs[n])
                if masked[n]:
                    a = jnp.where(diag_mask, a, 0.0)
                parts[n][b] = a.astype(BF16)
                runs[n] = runs[n] + cs[n][:, 0:1]
        outs = []
        for n in range(nch):
            a_c = parts[n][0] if nblks[n] == 1 else jnp.concatenate(parts[n], axis=1)
            outs.append(lax.dot_general(a_c, vt_c[:, :nblks[n] * ATT_BLK], NT_DIMS, preferred_element_type=F32))
        return outs, runs

    outs, runs = chunk_pass(pl.multiple_of(qi * ATT_TQ, ATT_TQ), [n // 2 + 1 for n in range(nch)],
                            [jnp.zeros((ATT_BLK, 1), F32)] * nch, True)
    state = [x for n in range(nch) for x in (outs[n], runs[n])]

    def body(it, st):
        outs, runs = chunk_pass(pl.multiple_of((qi - 1 - it) * ATT_TQ, ATT_TQ), [nsub] * nch,
                                [st[2 * n + 1] for n in range(nch)], False)
        return tuple(x for n in range(nch) for x in (st[2 * n] + outs[n], runs[n]))

    st = lax.fori_loop(0, qi, body, tuple(state))
    for sq in range(nsub):
        o_ref[0, sq * ATT_BLK:(sq + 1) * ATT_BLK, :] = jnp.where(
            lane < HEAD_DIM, st[4 * sq], st[4 * sq + 2]).astype(o_ref.dtype)


def _attn_prompt(q3, ktb, vtb, bias2):
    bsz, t_len, _ = q3.shape
    qblk = pl.BlockSpec((1, ATT_TQ, LANES), lambda b, hp, qi: (b, qi, hp))
    kvblk = pl.BlockSpec((1, LANES, t_len), lambda b, hp, qi: (b, hp, 0))
    o = pl.pallas_call(
        _attn_prompt_kernel,
        grid=(bsz, D_MODEL // LANES, t_len // ATT_TQ),
        in_specs=[pl.BlockSpec(memory_space=pltpu.SMEM), qblk, kvblk, kvblk],
        out_specs=qblk,
        out_shape=jax.ShapeDtypeStruct((bsz, t_len, D_MODEL), BF16),
        compiler_params=_cparams("arbitrary", "arbitrary", "arbitrary"),
        name="attn_prompt",
    )(bias2, q3, ktb, vtb)
    return o.reshape(bsz * t_len, D_MODEL)


SAMPLE_PAGES_PER_STEP = 8


def _attn_sample_kernel(pt_ref, q_ref, kn_ref, vn_ref, bias_ref, *refs):
    npg = SAMPLE_PAGES_PER_STEP
    k_refs = refs[:npg]
    v_refs = refs[npg:2 * npg]
    o_ref = refs[2 * npg]
    qbd_ref, acc_ref, carry_ref = refs[2 * npg + 1:]
    j = pl.program_id(1)
    t_new = q_ref.shape[1]
    rows = t_new * N_HEADS
    lane = lax.broadcasted_iota(jnp.int32, (N_HEADS, D_MODEL), 1)
    sub = lax.broadcasted_iota(jnp.int32, (N_HEADS, D_MODEL), 0)
    head_lanes = (lane >= sub * HEAD_DIM) & (lane < (sub + 1) * HEAD_DIM)
    bias = bias_ref[...]

    @pl.when(j == 0)
    def _():
        q = q_ref[0].astype(F32)
        qbd = jnp.concatenate(
            [jnp.where(head_lanes, jnp.broadcast_to(q[i:i + 1, :], (N_HEADS, D_MODEL)), 0.0) for i in range(t_new)],
            axis=0)
        qbd_ref[...] = qbd.astype(BF16)
        qi = lax.broadcasted_iota(jnp.int32, (rows, 1), 0) // N_HEADS
        kn = kn_ref[0]
        vn = vn_ref[0]
        carry = jnp.zeros((rows, 1), F32)
        acc = jnp.zeros((rows, D_MODEL), F32)
        for m in range(t_new - 1, -1, -1):
            valid = qi > m
            z2 = jnp.sum(qbd * kn[m:m + 1, :], axis=-1, keepdims=True) + bias
            sp = jnp.where(valid, _softplus2(z2), 0.0)
            a = jnp.where(valid, jnp.exp2(z2 - sp + carry), 0.0)
            acc = acc + a * vn[m:m + 1, :]
            carry = carry - sp
        acc_ref[...] = acc
        carry_ref[...] = carry

    neg_ge = _neg_suffix_ones(PAGE_SIZE)
    qbd = qbd_ref[...]
    acc = acc_ref[...]
    carry = carry_ref[...]
    zs = [jnp.dot(qbd, k_refs[r][0].astype(BF16), preferred_element_type=F32) + bias for r in range(npg)]
    cs = [jnp.dot(_softplus2(z2).astype(BF16), neg_ge, preferred_element_type=F32) for z2 in zs]
    weights = []
    for r in range(npg):
        weights.append(jnp.exp2(zs[r] + cs[r] + carry).astype(BF16))
        carry = carry + cs[r][:, 0:1]
    for r in range(npg):
        acc = acc + lax.dot_general(weights[r], v_refs[r][0].astype(BF16), NT_DIMS, preferred_element_type=F32)
    acc_ref[...] = acc
    carry_ref[...] = carry

    @pl.when(j == pl.num_programs(1) - 1)
    def _():
        for i in range(t_new):
            blk = jnp.where(head_lanes, acc[i * N_HEADS:(i + 1) * N_HEADS, :], 0.0)
            o_ref[0, i:i + 1, :] = jnp.sum(blk, axis=0, keepdims=True).astype(o_ref.dtype)


def _attn_sample(q, k_new, v_new, cache_kt, cache_vt, page_table, bias2):
    bsz, t_new, _ = q.shape
    n_pages = page_table.shape[1]
    npg = SAMPLE_PAGES_PER_STEP
    steps = n_pages // npg
    rows = t_new * N_HEADS
    bias_col = jnp.tile(bias2, t_new).reshape(rows, 1)
    tok = pl.BlockSpec((1, t_new, D_MODEL), lambda b, j, pt: (b, 0, 0))

    def page_spec(r):
        return pl.BlockSpec((1, D_MODEL, PAGE_SIZE), lambda b, j, pt: (pt[b, n_pages - 1 - j * npg - r], 0, 0))

    grid_spec = pltpu.PrefetchScalarGridSpec(
        num_scalar_prefetch=1,
        grid=(bsz, steps),
        in_specs=[tok, tok, tok, pl.BlockSpec((rows, 1), lambda b, j, pt: (0, 0))]
        + [page_spec(r) for r in range(npg)] + [page_spec(r) for r in range(npg)],
        out_specs=tok,
        scratch_shapes=[
            pltpu.VMEM((rows, D_MODEL), BF16),
            pltpu.VMEM((rows, D_MODEL), F32),
            pltpu.VMEM((rows, 1), F32),
        ],
    )
    return pl.pallas_call(
        _attn_sample_kernel,
        grid_spec=grid_spec,
        out_shape=jax.ShapeDtypeStruct((bsz, t_new, D_MODEL), BF16),
        compiler_params=_cparams("arbitrary", "arbitrary"),
        name="attn_sample",
    )(page_table, q, k_new, v_new, bias_col, *([cache_kt] * npg), *([cache_vt] * npg))


def _router_params(w_coarse, b_coarse, w_fine, b_fine):
    wf = jnp.transpose(w_fine, (1, 0, 2)).reshape(D_MODEL, N_EXPERTS)
    w = jnp.concatenate([wf, w_coarse, jnp.zeros((D_MODEL, LANES - N_EXPERTS - N_GROUPS), F32)], axis=1)
    b = jnp.concatenate([b_fine.reshape(-1), b_coarse, jnp.zeros((LANES - N_EXPERTS - N_GROUPS,), F32)])
    return w, b.reshape(1, LANES)


def _ffn(hn, res, mw, g_final, tm, tm_rows, tm_moe, final_norm, sorted_buf=None):
    w_r, b_r, wg, wu, wd, layer = mw
    n = hn.shape[0]
    idx, gate, cnt = _router(hn, w_r, b_r, tm)
    n_tiles = pl.cdiv(TOP_K * n, tm_moe) + N_EXPERTS
    pos, tile_expert = _route_plan(idx, cnt, tm, tm_moe, n_tiles)
    if sorted_buf is None:
        sorted_buf = jnp.zeros((n_tiles * tm_moe, D_MODEL), F32)
    xs = _dispatch(hn, pos, sorted_buf, tm_rows)
    ys = _moe_sorted(xs, tile_expert, wg, wu, wd, layer, tm_moe)
    return _combine(ys, pos, res, gate, g_final, tm_rows, final_norm), ys


def kernel(x_prompt, x_sample, state_conv, cache_k, cache_v, page_table, norm_mix, norm_ffn, norm_final,
           conv_w_pw1, conv_b_pw1, conv_w_dw, conv_b_dw, conv_ln_g, conv_ln_b, conv_w_pw2, conv_b_pw2,
           sb_w_qkv, sb_w_o, sb_logit_bias, moe_w_coarse, moe_b_coarse, moe_w_fine, moe_b_fine,
           moe_w_gate, moe_w_up, moe_w_down):
    bsz, t_len, _ = x_prompt.shape
    dbsz, t_new, _ = x_sample.shape
    n_p = bsz * t_len
    n_s = dbsz * t_new
    tm_p, tm_s = 512, n_s
    tmm_p, tmm_s = 256, 128
    tmr_p, tmr_s = 1024, n_s
    vec = lambda a: a.reshape(1, -1)
    g_fin = vec(norm_final)

    xp = x_prompt.reshape(n_p, D_MODEL)
    xs = x_sample.reshape(n_s, D_MODEL)

    moe_w = []
    for i in range(2):
        w_r, b_r = _router_params(moe_w_coarse[i], moe_b_coarse[i], moe_w_fine[i], moe_b_fine[i])
        moe_w.append((w_r, b_r, moe_w_gate, moe_w_up, moe_w_down, i))

    w1 = conv_w_pw1[0].astype(BF16)
    w2 = conv_w_pw2[0].astype(BF16)
    up = _pw1_glu(xp, vec(norm_mix[0]), w1, vec(conv_b_pw1[0]), tm_p)
    us = _pw1_glu(xs, vec(norm_mix[0]), w1, vec(conv_b_pw1[0]), tm_s)
    up3 = up.reshape(bsz, t_len, D_MODEL)
    full_s = jnp.concatenate([state_conv[0], us.reshape(dbsz, t_new, D_MODEL)], axis=1)
    conv_state_p = up3[:, t_len - (CONV_WIDTH - 1):, :][None]
    conv_state_s = full_s[:, t_new:, :][None]
    cp = _conv_prompt(up3, conv_w_dw[0], vec(conv_b_dw[0]), 256).reshape(n_p, D_MODEL)
    cs = _conv_sample(full_s, conv_w_dw[0], vec(conv_b_dw[0]), 8).reshape(n_s, D_MODEL)
    ln = (vec(conv_ln_g[0]), vec(conv_ln_b[0]), vec(norm_ffn[0]))
    yp, hp = _proj(cp, w2, vec(conv_b_pw2[0]), xp, *ln, tm_p, True)
    ys, hs = _proj(cs, w2, vec(conv_b_pw2[0]), xs, *ln, tm_s, True)
    yp, buf_p = _ffn(hp, yp, moe_w[0], g_fin, tm_p, tmr_p, tmm_p, False)
    ys, buf_s = _ffn(hs, ys, moe_w[0], g_fin, tm_s, tmr_s, tmm_s, False)

    wqkv = sb_w_qkv[0].astype(BF16)
    wo = sb_w_o[0].astype(BF16)
    bias2 = sb_logit_bias[0].astype(F32) * LOG2E
    qp, ktp, vtp, ktb, vtb = _qkv_t(yp.reshape(bsz, t_len, D_MODEL), vec(norm_mix[1]), wqkv[:, :D_MODEL],
                                    wqkv[:, D_MODEL:2 * D_MODEL].T, wqkv[:, 2 * D_MODEL:].T, tm_p)
    qs, ks, vs = _qkv(ys, vec(norm_mix[1]), wqkv, tm_s)
    op = _attn_prompt(qp, ktb, vtb, bias2)
    n_pool = cache_k.shape[1]
    cache_kt = jnp.transpose(cache_k[0], (0, 2, 3, 1)).reshape(n_pool, D_MODEL, PAGE_SIZE)
    cache_vt = jnp.transpose(cache_v[0], (0, 2, 3, 1)).reshape(n_pool, D_MODEL, PAGE_SIZE)
    os_ = _attn_sample(qs.reshape(dbsz, t_new, D_MODEL), ks.reshape(dbsz, t_new, D_MODEL),
                       vs.reshape(dbsz, t_new, D_MODEL), cache_kt, cache_vt,
                       page_table, bias2).reshape(n_s, D_MODEL)
    zero_b = jnp.zeros((1, D_MODEL), F32)
    nf1 = vec(norm_ffn[1])
    yp, hp = _proj(op, wo, zero_b, yp, nf1, nf1, nf1, tm_p, False)
    ys, hs = _proj(os_, wo, zero_b, ys, nf1, nf1, nf1, tm_s, False)
    yp, _ = _ffn(hp, yp, moe_w[1], g_fin, tm_p, tmr_p, tmm_p, True, buf_p)
    ys, _ = _ffn(hs, ys, moe_w[1], g_fin, tm_s, tmr_s, tmm_s, True, buf_s)

    def rows_p(xt):
        return jnp.transpose(xt.reshape(bsz, N_HEADS, HEAD_DIM, t_len), (0, 3, 1, 2))[None]

    hshape_s = (1, dbsz, t_new, N_HEADS, HEAD_DIM)
    return (yp.reshape(bsz, t_len, D_MODEL), ys.reshape(dbsz, t_new, D_MODEL), conv_state_p, conv_state_s,
            rows_p(ktp), rows_p(vtp), ks.reshape(hshape_s), vs.reshape(hshape_s))
```

```python
import functools

import jax
import jax.numpy as jnp
from jax import lax
from jax.experimental import pallas as pl
from jax.experimental.pallas import tpu as pltpu

D_MODEL = 1024
CONV_WIDTH = 31
N_HEADS = 16
HEAD_DIM = D_MODEL // N_HEADS
N_GROUPS = 4
EXPERTS_PER_GROUP = 8
N_EXPERTS = N_GROUPS * EXPERTS_PER_GROUP
D_EXPERT = D_MODEL // 4
PAGE_SIZE = 128
NORM_EPS = 1e-6
LN_EPS = 1e-5

LANES = 128
SUBLANES = 8
HIST_PAD = 32
VMEM_LIMIT = 56 * 1024 * 1024

F32 = jnp.float32
BF16 = jnp.bfloat16


def _cparams(*sem):
    return pltpu.CompilerParams(dimension_semantics=sem, vmem_limit_bytes=VMEM_LIMIT)


def _rms(x, g):
    return x * lax.rsqrt(jnp.mean(x * x, axis=-1, keepdims=True) + NORM_EPS) * g


LOG2E = 1.4426950408889634
POW2_CLAMP = 64.0


def _softplus2(z2):
    return jnp.where(z2 > POW2_CLAMP, z2, jnp.log2(1.0 + jnp.exp2(jnp.minimum(z2, POW2_CLAMP))))


def _pw1_glu_kernel(x_ref, g_ref, w_ref, b_ref, u_ref):
    h = _rms(x_ref[...], g_ref[...]).astype(BF16)
    a = jnp.dot(h, w_ref[...], preferred_element_type=F32) + b_ref[...]
    u_ref[...] = a[:, :D_MODEL] * jax.nn.sigmoid(a[:, D_MODEL:])


def _pw1_glu(x, g, w_bf, b, tm):
    n = x.shape[0]
    return pl.pallas_call(
        _pw1_glu_kernel,
        grid=(n // tm,),
        in_specs=[
            pl.BlockSpec((tm, D_MODEL), lambda i: (i, 0)),
            pl.BlockSpec((1, D_MODEL), lambda i: (0, 0)),
            pl.BlockSpec((D_MODEL, 2 * D_MODEL), lambda i: (0, 0)),
            pl.BlockSpec((1, 2 * D_MODEL), lambda i: (0, 0)),
        ],
        out_specs=pl.BlockSpec((tm, D_MODEL), lambda i: (i, 0)),
        out_shape=jax.ShapeDtypeStruct((n, D_MODEL), F32),
        compiler_params=_cparams("arbitrary"),
        name="pw1_glu",
    )(x, g, w_bf, b)


Q_SCALE2 = (HEAD_DIM ** -0.5) * LOG2E


def _qkv_kernel(x_ref, g_ref, w_ref, q_ref, k_ref, v_ref):
    h = _rms(x_ref[...], g_ref[...]).astype(BF16)
    a = jnp.dot(h, w_ref[...], preferred_element_type=F32)
    q_ref[...] = (a[:, :D_MODEL] * Q_SCALE2).astype(BF16)
    k_ref[...] = a[:, D_MODEL:2 * D_MODEL]
    v_ref[...] = a[:, 2 * D_MODEL:]


def _qkv(x, g, w_bf, tm):
    n = x.shape[0]
    row = pl.BlockSpec((tm, D_MODEL), lambda i: (i, 0))
    return pl.pallas_call(
        _qkv_kernel,
        grid=(n // tm,),
        in_specs=[
            row,
            pl.BlockSpec((1, D_MODEL), lambda i: (0, 0)),
            pl.BlockSpec((D_MODEL, 3 * D_MODEL), lambda i: (0, 0)),
        ],
        out_specs=[row, row, row],
        out_shape=[
            jax.ShapeDtypeStruct((n, D_MODEL), BF16),
            jax.ShapeDtypeStruct((n, D_MODEL), F32),
            jax.ShapeDtypeStruct((n, D_MODEL), F32),
        ],
        compiler_params=_cparams("arbitrary"),
        name="qkv",
    )(x, g, w_bf)


def _qkv_t_kernel(x_ref, g_ref, wq_ref, wkt_ref, wvt_ref, q_ref, kt_ref, vt_ref, ktb_ref, vtb_ref):
    h = _rms(x_ref[0], g_ref[...]).astype(BF16)
    q_ref[0] = (jnp.dot(h, wq_ref[...], preferred_element_type=F32) * Q_SCALE2).astype(BF16)
    nt = (((1,), (1,)), ((), ()))
    kt = lax.dot_general(wkt_ref[...], h, nt, preferred_element_type=F32)
    vt = lax.dot_general(wvt_ref[...], h, nt, preferred_element_type=F32)
    kt_ref[0] = kt
    vt_ref[0] = vt
    ktb_ref[0] = kt.astype(BF16)
    vtb_ref[0] = vt.astype(BF16)


def _qkv_t(x3, g, wq_bf, wkt_bf, wvt_bf, tm):
    bsz, t_len, _ = x3.shape
    row = pl.BlockSpec((1, tm, D_MODEL), lambda b, t: (b, t, 0))
    col = pl.BlockSpec((1, D_MODEL, tm), lambda b, t: (b, 0, t))
    wspec = pl.BlockSpec((D_MODEL, D_MODEL), lambda b, t: (0, 0))
    tshape = (bsz, D_MODEL, t_len)
    return pl.pallas_call(
        _qkv_t_kernel,
        grid=(bsz, t_len // tm),
        in_specs=[row, pl.BlockSpec((1, D_MODEL), lambda b, t: (0, 0)), wspec, wspec, wspec],
        out_specs=[row, col, col, col, col],
        out_shape=[
            jax.ShapeDtypeStruct((bsz, t_len, D_MODEL), BF16),
            jax.ShapeDtypeStruct(tshape, F32),
            jax.ShapeDtypeStruct(tshape, F32),
            jax.ShapeDtypeStruct(tshape, BF16),
            jax.ShapeDtypeStruct(tshape, BF16),
        ],
        compiler_params=_cparams("arbitrary", "arbitrary"),
        name="qkv_t",
    )(x3, g, wq_bf, wkt_bf, wvt_bf)


CONV_ROWS = 32


def _conv_prompt_kernel(u_ref, w_ref, b_ref, c_ref, sh_ref):
    tt = u_ref.shape[1]
    rows = HIST_PAD + tt
    t = pl.program_id(1)

    @pl.when(t == 0)
    def _():
        sh_ref[0, 0:HIST_PAD, :] = jnp.zeros((HIST_PAD, D_MODEL), F32)

    sh_ref[0, HIST_PAD:rows, :] = u_ref[0]
    for s in range(1, SUBLANES):
        sh_ref[s] = pltpu.roll(sh_ref[0], rows - s, axis=0)
    first = HIST_PAD - (CONV_WIDTH - 1)
    bias = b_ref[...]

    def body(r, carry):
        r0 = pl.multiple_of(r * CONV_ROWS, CONV_ROWS)
        accs = [jnp.zeros((SUBLANES, D_MODEL), F32) + bias for _ in range(CONV_ROWS // SUBLANES)]
        for j in range(CONV_WIDTH):
            s, a = (first + j) % SUBLANES, (first + j) // SUBLANES
            w = w_ref[j * SUBLANES:(j + 1) * SUBLANES, :]
            for i in range(len(accs)):
                accs[i] = accs[i] + sh_ref[s, pl.ds(r0 + (a + i) * SUBLANES, SUBLANES), :] * w
        for i, acc in enumerate(accs):
            c_ref[0, pl.ds(r0 + i * SUBLANES, SUBLANES), :] = acc
        return carry

    lax.fori_loop(0, tt // CONV_ROWS, body, 0)
    sh_ref[0, 0:HIST_PAD, :] = sh_ref[0, tt:rows, :]


def _conv_prompt(u, w_dw, b_dw, tt):
    bsz, t_len, _ = u.shape
    blk = pl.BlockSpec((1, tt, D_MODEL), lambda b, t: (b, t, 0))
    w_rep = jnp.repeat(w_dw, SUBLANES, axis=0)
    return pl.pallas_call(
        _conv_prompt_kernel,
        grid=(bsz, t_len // tt),
        in_specs=[
            blk,
            pl.BlockSpec((CONV_WIDTH * SUBLANES, D_MODEL), lambda b, t: (0, 0)),
            pl.BlockSpec((1, D_MODEL), lambda b, t: (0, 0)),
        ],
        out_specs=blk,
        out_shape=jax.ShapeDtypeStruct(u.shape, F32),
        scratch_shapes=[pltpu.VMEM((SUBLANES, HIST_PAD + tt, D_MODEL), F32)],
        compiler_params=_cparams("arbitrary", "arbitrary"),
        name="conv_prompt",
    )(u, w_rep, b_dw)


def _conv_sample_kernel(full_ref, w_ref, b_ref, c_ref):
    bb = full_ref.shape[0]
    t_new = c_ref.shape[1]
    w = w_ref[...]
    bias = b_ref[...]
    for b in range(bb):
        for i in range(t_new):
            win = full_ref[b, i:i + CONV_WIDTH, :]
            c_ref[b, i:i + 1, :] = jnp.sum(win * w, axis=0, keepdims=True) + bias


def _conv_sample(full, w_dw, b_dw, bb):
    bsz, rows, _ = full.shape
    t_new = rows - (CONV_WIDTH - 1)
    return pl.pallas_call(
        _conv_sample_kernel,
        grid=(bsz // bb,),
        in_specs=[
            pl.BlockSpec((bb, rows, D_MODEL), lambda i: (i, 0, 0)),
            pl.BlockSpec((CONV_WIDTH, D_MODEL), lambda i: (0, 0)),
            pl.BlockSpec((1, D_MODEL), lambda i: (0, 0)),
        ],
        out_specs=pl.BlockSpec((bb, t_new, D_MODEL), lambda i: (i, 0, 0)),
        out_shape=jax.ShapeDtypeStruct((bsz, t_new, D_MODEL), F32),
        compiler_params=_cparams("arbitrary"),
        name="conv_sample",
    )(full, w_dw, b_dw)


def _proj_kernel(a_ref, w_ref, b_ref, res_ref, lng_ref, lnb_ref, gf_ref, y_ref, hn_ref, *, pre_ln):
    a = a_ref[...]
    if pre_ln:
        a = a.astype(F32)
        mu = jnp.mean(a, axis=-1, keepdims=True)
        ac = a - mu
        var = jnp.mean(ac * ac, axis=-1, keepdims=True)
        a = ac * lax.rsqrt(var + LN_EPS) * lng_ref[...] + lnb_ref[...]
        a = a * jax.nn.sigmoid(a)
    y = res_ref[...] + jnp.dot(a.astype(BF16), w_ref[...], preferred_element_type=F32) + b_ref[...]
    y_ref[...] = y
    hn_ref[...] = _rms(y, gf_ref[...])


def _proj(a, w_bf, b, res, ln_g, ln_b, g_ffn, tm, pre_ln):
    n = a.shape[0]
    row = pl.BlockSpec((tm, D_MODEL), lambda i: (i, 0))
    vec = pl.BlockSpec((1, D_MODEL), lambda i: (0, 0))
    return pl.pallas_call(
        functools.partial(_proj_kernel, pre_ln=pre_ln),
        grid=(n // tm,),
        in_specs=[row, pl.BlockSpec((D_MODEL, D_MODEL), lambda i: (0, 0)), vec, row, vec, vec, vec],
        out_specs=[row, row],
        out_shape=[jax.ShapeDtypeStruct((n, D_MODEL), F32), jax.ShapeDtypeStruct((n, D_MODEL), F32)],
        compiler_params=_cparams("arbitrary"),
        name="proj_ln" if pre_ln else "proj",
    )(a, w_bf, b, res, ln_g, ln_b, g_ffn)


def _router_kernel(hn_ref, w_ref, b_ref, idx_ref, gate_ref, cnt_ref):
    logits = jnp.dot(hn_ref[...], w_ref[...], preferred_element_type=F32,
                     precision=lax.Precision.HIGHEST) + b_ref[...]
    lane = lax.broadcasted_iota(jnp.int32, logits.shape, 1)
    neg = jnp.float32(-jnp.inf)
    cmask = (lane >= N_EXPERTS) & (lane < N_EXPERTS + N_GROUPS)
    cl = jnp.where(cmask, logits, neg)
    cmax = jnp.max(cl, axis=-1, keepdims=True)
    g_lane = jnp.min(jnp.where(cl == cmax, lane, LANES), axis=-1, keepdims=True)
    g_top = 1.0 / jnp.sum(jnp.exp(cl - cmax), axis=-1, keepdims=True)
    lo = (g_lane - N_EXPERTS) * EXPERTS_PER_GROUP
    fmask = (lane >= lo) & (lane < lo + EXPERTS_PER_GROUP)
    fl = jnp.where(fmask, logits, neg)
    m1 = jnp.max(fl, axis=-1, keepdims=True)
    i1 = jnp.min(jnp.where(fl == m1, lane, LANES), axis=-1, keepdims=True)
    fl2 = jnp.where(lane == i1, neg, fl)
    m2 = jnp.max(fl2, axis=-1, keepdims=True)
    i2 = jnp.min(jnp.where(fl2 == m2, lane, LANES), axis=-1, keepdims=True)
    r = jnp.exp(m2 - m1)
    w1 = 1.0 / (1.0 + r)
    w2 = r * w1
    tm = logits.shape[0]
    onehot = jnp.where((lane == i1) | (lane == i2), 1.0, 0.0)
    tr = lax.broadcasted_iota(jnp.int32, (tm, tm), 0)
    tc = lax.broadcasted_iota(jnp.int32, (tm, tm), 1)
    before = jnp.where(tc < tr, 1.0, 0.0).astype(BF16)

    @pl.when(pl.program_id(0) == 0)
    def _():
        cnt_ref[...] = jnp.zeros_like(cnt_ref)

    prior = jnp.dot(before, onehot.astype(BF16), preferred_element_type=F32) + cnt_ref[...]
    rank1 = jnp.sum(jnp.where(lane == i1, prior, 0.0), axis=-1, keepdims=True)
    rank2 = jnp.sum(jnp.where(lane == i2, prior, 0.0), axis=-1, keepdims=True)
    cnt_ref[...] += jnp.sum(onehot, axis=0, keepdims=True)
    idx = jnp.where(lane == 0, i1, jnp.where(lane == 1, i2, jnp.where(
        lane == 2, rank1.astype(jnp.int32), jnp.where(lane == 3, rank2.astype(jnp.int32), 0))))
    idx_ref[...] = idx
    gate_ref[...] = g_top * jnp.where(lane == 0, w1, jnp.where(lane == 1, w2, 0.0))


def _router(hn, w_r, b_r, tm):
    n = hn.shape[0]
    row = pl.BlockSpec((tm, LANES), lambda i: (i, 0))
    return pl.pallas_call(
        _router_kernel,
        grid=(n // tm,),
        in_specs=[
            pl.BlockSpec((tm, D_MODEL), lambda i: (i, 0)),
            pl.BlockSpec((D_MODEL, LANES), lambda i: (0, 0)),
            pl.BlockSpec((1, LANES), lambda i: (0, 0)),
        ],
        out_specs=[row, row, pl.BlockSpec((1, LANES), lambda i: (0, 0))],
        out_shape=[jax.ShapeDtypeStruct((n, LANES), jnp.int32), jax.ShapeDtypeStruct((n, LANES), F32),
                   jax.ShapeDtypeStruct((1, LANES), F32)],
        compiler_params=_cparams("arbitrary"),
        name="router",
    )(hn, w_r, b_r)


TOP_K = 2


def _row_copy_wait(src, dst, sem, rows):
    pltpu.make_async_copy(src.at[pl.ds(0, rows), :], dst.at[pl.ds(0, rows), :], sem).wait()


def _dispatch_kernel(pos_ref, x_ref, xs_init_ref, xs_ref, sem):
    del xs_init_ref
    tm = x_ref.shape[0]
    base = pl.program_id(0) * tm

    def issue(t, carry):
        for k in range(TOP_K):
            p = pos_ref[TOP_K * (base + t) + k]
            pltpu.make_async_copy(x_ref.at[pl.ds(t, 1), :], xs_ref.at[pl.ds(p, 1), :], sem).start()
        return carry

    lax.fori_loop(0, tm, issue, 0, unroll=4)
    for k in range(TOP_K):
        _row_copy_wait(x_ref, xs_ref, sem, tm)


def _dispatch(x, pos, xs_init, tm):
    n = x.shape[0]
    n_rows = xs_init.shape[0]
    grid_spec = pltpu.PrefetchScalarGridSpec(
        num_scalar_prefetch=1,
        grid=(n // tm,),
        in_specs=[pl.BlockSpec((tm, D_MODEL), lambda i, pos: (i, 0)), pl.BlockSpec(memory_space=pl.ANY)],
        out_specs=pl.BlockSpec(memory_space=pl.ANY),
        scratch_shapes=[pltpu.SemaphoreType.DMA(())],
    )
    return pl.pallas_call(
        _dispatch_kernel,
        grid_spec=grid_spec,
        out_shape=jax.ShapeDtypeStruct((n_rows, D_MODEL), F32),
        input_output_aliases={2: 0},
        compiler_params=_cparams("arbitrary"),
        name="moe_dispatch",
    )(pos, x, xs_init)


def _moe_sorted_kernel(te_ref, xs_ref, wg_ref, wu_ref, wd_ref, ys_ref, wgb_ref, wub_ref, wdb_ref):
    i = pl.program_id(0)

    @pl.when((i == 0) | (te_ref[i] != te_ref[jnp.maximum(i - 1, 0)]))
    def _():
        wgb_ref[...] = wg_ref[0, 0].astype(BF16)
        wub_ref[...] = wu_ref[0, 0].astype(BF16)
        wdb_ref[...] = wd_ref[0, 0].astype(BF16)

    x = xs_ref[...].astype(BF16)
    hg = jnp.dot(x, wgb_ref[...], preferred_element_type=F32)
    hu = jnp.dot(x, wub_ref[...], preferred_element_type=F32)
    h = hg * jax.nn.sigmoid(hg) * hu
    ys_ref[...] = jnp.dot(h.astype(BF16), wdb_ref[...], preferred_element_type=F32)


def _moe_sorted(xs, tile_expert, w_gate, w_up, w_down, layer, tm):
    n_rows = xs.shape[0]
    row = pl.BlockSpec((tm, D_MODEL), lambda i, te: (i, 0))
    grid_spec = pltpu.PrefetchScalarGridSpec(
        num_scalar_prefetch=1,
        grid=(n_rows // tm,),
        in_specs=[
            row,
            pl.BlockSpec((1, 1, D_MODEL, D_EXPERT), lambda i, te: (layer, te[i], 0, 0)),
            pl.BlockSpec((1, 1, D_MODEL, D_EXPERT), lambda i, te: (layer, te[i], 0, 0)),
            pl.BlockSpec((1, 1, D_EXPERT, D_MODEL), lambda i, te: (layer, te[i], 0, 0)),
        ],
        out_specs=row,
        scratch_shapes=[pltpu.VMEM((D_MODEL, D_EXPERT), BF16), pltpu.VMEM((D_MODEL, D_EXPERT), BF16),
                        pltpu.VMEM((D_EXPERT, D_MODEL), BF16)],
    )
    return pl.pallas_call(
        _moe_sorted_kernel,
        grid_spec=grid_spec,
        out_shape=jax.ShapeDtypeStruct((n_rows, D_MODEL), F32),
        compiler_params=_cparams("arbitrary"),
        name="moe_experts",
    )(tile_expert, xs, w_gate, w_up, w_down)


def _combine_kernel(pos_ref, res_ref, gate_ref, gfin_ref, ys_ref, y_ref, buf_ref, sem, *, final_norm):
    tm = res_ref.shape[0]
    base = pl.program_id(0) * tm

    def issue(t, carry):
        for k in range(TOP_K):
            p = pos_ref[TOP_K * (base + t) + k]
            pltpu.make_async_copy(ys_ref.at[pl.ds(p, 1), :], buf_ref.at[k, pl.ds(t, 1), :], sem).start()
        return carry

    lax.fori_loop(0, tm, issue, 0, unroll=4)
    for k in range(TOP_K):
        _row_copy_wait(ys_ref, buf_ref.at[k], sem, tm)
    g = gate_ref[...]
    y = res_ref[...] + g[:, 0:1] * buf_ref[0] + g[:, 1:2] * buf_ref[1]
    if final_norm:
        y = _rms(y, gfin_ref[...])
    y_ref[...] = y


def _combine(ys, pos, res, gate, g_final, tm, final_norm):
    n = res.shape[0]
    row = pl.BlockSpec((tm, D_MODEL), lambda i, pos: (i, 0))
    grid_spec = pltpu.PrefetchScalarGridSpec(
        num_scalar_prefetch=1,
        grid=(n // tm,),
        in_specs=[row, pl.BlockSpec((tm, LANES), lambda i, pos: (i, 0)),
                  pl.BlockSpec((1, D_MODEL), lambda i, pos: (0, 0)), pl.BlockSpec(memory_space=pl.ANY)],
        out_specs=row,
        scratch_shapes=[pltpu.VMEM((TOP_K, tm, D_MODEL), F32), pltpu.SemaphoreType.DMA(())],
    )
    return pl.pallas_call(
        functools.partial(_combine_kernel, final_norm=final_norm),
        grid_spec=grid_spec,
        out_shape=jax.ShapeDtypeStruct((n, D_MODEL), F32),
        compiler_params=_cparams("arbitrary"),
        name="moe_combine_final" if final_norm else "moe_combine",
    )(pos, res, gate, g_final, ys)


def _pos_kernel(idx_ref, starts_ref, pos_ref):
    idx = idx_ref[...]
    lane = lax.broadcasted_iota(jnp.int32, idx.shape, 1)
    starts = starts_ref[...]
    pos = []
    for k in range(TOP_K):
        first = jnp.sum(jnp.where(lane == idx[:, k:k + 1], starts, 0.0), axis=-1, keepdims=True)
        pos.append(first.astype(jnp.int32) + idx[:, TOP_K + k:TOP_K + k + 1])
    pos_ref[...] = jnp.where(lane == 0, pos[0], jnp.where(lane == 1, pos[1], 0))


def _route_plan(idx, cnt, tm, tm_moe, n_tiles):
    n = idx.shape[0]
    counts = cnt[0].astype(jnp.int32)
    padded = (counts + tm_moe - 1) // tm_moe * tm_moe
    ends = jnp.cumsum(padded)
    starts = (ends - padded).astype(F32).reshape(1, LANES)
    row = pl.BlockSpec((tm, LANES), lambda i: (i, 0))
    pos = pl.pallas_call(
        _pos_kernel,
        grid=(n // tm,),
        in_specs=[row, pl.BlockSpec((1, LANES), lambda i: (0, 0))],
        out_specs=row,
        out_shape=jax.ShapeDtypeStruct((n, LANES), jnp.int32),
        compiler_params=_cparams("arbitrary"),
        name="moe_pos",
    )(idx, starts)[:, :TOP_K].reshape(-1)
    tile_start = jnp.arange(n_tiles, dtype=jnp.int32) * tm_moe
    tile_expert = jnp.sum((ends[None, :N_EXPERTS] <= tile_start[:, None]).astype(jnp.int32), axis=1)
    return pos, jnp.minimum(tile_expert, N_EXPERTS - 1)


ATT_BLK = 128
ATT_TQ = 512


NT_DIMS = (((1,), (1,)), ((), ()))


def _neg_suffix_ones(n):
    r = lax.broadcasted_iota(jnp.int32, (n, n), 0)
    c = lax.broadcasted_iota(jnp.int32, (n, n), 1)
    return jnp.where(r >= c, -1.0, 0.0).astype(BF16)


def _attn_prompt_kernel(bias2_ref, q_ref, kt_ref, vt_ref, o_ref):
    hp = pl.program_id(1)
    qi = pl.program_id(2)
    nsub = ATT_TQ // ATT_BLK
    neg_ge = _neg_suffix_ones(ATT_BLK)
    lane = lax.broadcasted_iota(jnp.int32, (ATT_BLK, LANES), 1)
    row = lax.broadcasted_iota(jnp.int32, (ATT_BLK, ATT_BLK), 0)
    col = lax.broadcasted_iota(jnp.int32, (ATT_BLK, ATT_BLK), 1)
    diag_mask = col < row
    bias2 = [bias2_ref[2 * hp + h] for h in range(2)]

    chains = []
    for sq in range(nsub):
        q = q_ref[0, sq * ATT_BLK:(sq + 1) * ATT_BLK, :]
        for h in range(2):
            head = (lane >= h * HEAD_DIM) & (lane < (h + 1) * HEAD_DIM)
            chains.append((jnp.where(head, q, jnp.zeros_like(q)), bias2[h]))
    nch = len(chains)

    def chunk_pass(c0, nblks, runs, diagonal):
        kt_c = kt_ref[0, :, pl.ds(c0, ATT_TQ)]
        vt_c = vt_ref[0, :, pl.ds(c0, ATT_TQ)]
        blk = lambda x, b: x[:, b * ATT_BLK:(b + 1) * ATT_BLK]
        zs = [jnp.dot(qh, kt_c[:, :nb * ATT_BLK], preferred_element_type=F32) + bias
              for (qh, bias), nb in zip(chains, nblks)]
        runs = list(runs)
        parts = [[None] * nb for nb in nblks]
        for b in range(max(nblks) - 1, -1, -1):
            live = [n for n in range(nch) if nblks[n] > b]
            masked = {n: diagonal and b == nblks[n] - 1 for n in live}
            sps = {}
            for n in live:
                sp = _softplus2(blk(zs[n], b))
                if masked[n]:
                    sp = jnp.where(diag_mask, sp, 0.0)
                sps[n] = sp.astype(BF16)
            cs = {n: jnp.dot(sps[n], neg_ge, preferred_element_type=F32) for n in live}
            for n in live:
                a = jnp.exp2(blk(zs[n], b) + cs[n] + runs[n])
                if masked[n]:
                    a = jnp.where(diag_mask, a, 0.0)
                parts[n][b] = a.astype(BF16)
                runs[n] = runs[n] + cs[n][:, 0:1]
        outs = []
        for n in range(nch):
            a_c = parts[n][0] if nblks[n] == 1 else jnp.concatenate(parts[n], axis=1)
            outs.append(lax.dot_general(a_c, vt_c[:, :nblks[n] * ATT_BLK], NT_DIMS, preferred_element_type=F32))
        return outs, runs

    outs, runs = chunk_pass(pl.multiple_of(qi * ATT_TQ, ATT_TQ), [n // 2 + 1 for n in range(nch)],
                            [jnp.zeros((ATT_BLK, 1), F32)] * nch, True)
    state = [x for n in range(nch) for x in (outs[n], runs[n])]

    def body(it, st):
        outs, runs = chunk_pass(pl.multiple_of((qi - 1 - it) * ATT_TQ, ATT_TQ), [nsub] * nch,
                                [st[2 * n + 1] for n in range(nch)], False)
        return tuple(x for n in range(nch) for x in (st[2 * n] + outs[n], runs[n]))

    st = lax.fori_loop(0, qi, body, tuple(state))
    for sq in range(nsub):
        o_ref[0, sq * ATT_BLK:(sq + 1) * ATT_BLK, :] = jnp.where(
            lane < HEAD_DIM, st[4 * sq], st[4 * sq + 2]).astype(o_ref.dtype)


def _attn_prompt(q3, ktb, vtb, bias2):
    bsz, t_len, _ = q3.shape
    qblk = pl.BlockSpec((1, ATT_TQ, LANES), lambda b, hp, qi: (b, qi, hp))
    kvblk = pl.BlockSpec((1, LANES, t_len), lambda b, hp, qi: (b, hp, 0))
    o = pl.pallas_call(
        _attn_prompt_kernel,
        grid=(bsz, D_MODEL // LANES, t_len // ATT_TQ),
        in_specs=[pl.BlockSpec(memory_space=pltpu.SMEM), qblk, kvblk, kvblk],
        out_specs=qblk,
        out_shape=jax.ShapeDtypeStruct((bsz, t_len, D_MODEL), BF16),
        compiler_params=_cparams("arbitrary", "arbitrary", "arbitrary"),
        name="attn_prompt",
    )(bias2, q3, ktb, vtb)
    return o.reshape(bsz * t_len, D_MODEL)


SAMPLE_PAGES_PER_STEP = 16


def _attn_sample_kernel(pt_ref, q_ref, kn_ref, vn_ref, bias_ref, *refs):
    npg = SAMPLE_PAGES_PER_STEP
    k_refs = refs[:npg]
    v_refs = refs[npg:2 * npg]
    o_ref = refs[2 * npg]
    qbd_ref, acc_ref, carry_ref = refs[2 * npg + 1:]
    j = pl.program_id(1)
    t_new = q_ref.shape[1]
    rows = t_new * N_HEADS
    lane = lax.broadcasted_iota(jnp.int32, (N_HEADS, D_MODEL), 1)
    sub = lax.broadcasted_iota(jnp.int32, (N_HEADS, D_MODEL), 0)
    head_lanes = (lane >= sub * HEAD_DIM) & (lane < (sub + 1) * HEAD_DIM)
    bias = bias_ref[...]

    @pl.when(j == 0)
    def _():
        q = q_ref[0].astype(F32)
        qbd = jnp.concatenate(
            [jnp.where(head_lanes, jnp.broadcast_to(q[i:i + 1, :], (N_HEADS, D_MODEL)), 0.0) for i in range(t_new)],
            axis=0)
        qbd_ref[...] = qbd.astype(BF16)
        qi = lax.broadcasted_iota(jnp.int32, (rows, 1), 0) // N_HEADS
        kn = kn_ref[0]
        vn = vn_ref[0]
        carry = jnp.zeros((rows, 1), F32)
        acc = jnp.zeros((rows, D_MODEL), F32)
        for m in range(t_new - 1, -1, -1):
            valid = qi > m
            z2 = jnp.sum(qbd * kn[m:m + 1, :], axis=-1, keepdims=True) + bias
            sp = jnp.where(valid, _softplus2(z2), 0.0)
            a = jnp.where(valid, jnp.exp2(z2 - sp + carry), 0.0)
            acc = acc + a * vn[m:m + 1, :]
            carry = carry - sp
        acc_ref[...] = acc
        carry_ref[...] = carry

    neg_ge = _neg_suffix_ones(PAGE_SIZE)
    qbd = qbd_ref[...]
    acc = acc_ref[...]
    carry = carry_ref[...]
    zs = [jnp.dot(qbd, k_refs[r][0].astype(BF16), preferred_element_type=F32) + bias for r in range(npg)]
    cs = [jnp.dot(_softplus2(z2).astype(BF16), neg_ge, preferred_element_type=F32) for z2 in zs]
    weights = []
    for r in range(npg):
        weights.append(jnp.exp2(zs[r] + cs[r] + carry).astype(BF16))
        carry = carry + cs[r][:, 0:1]
    for r in range(npg):
        acc = acc + lax.dot_general(weights[r], v_refs[r][0].astype(BF16), NT_DIMS, preferred_element_type=F32)
    acc_ref[...] = acc
    carry_ref[...] = carry

    @pl.when(j == pl.num_programs(1) - 1)
    def _():
        for i in range(t_new):
            blk = jnp.where(head_lanes, acc[i * N_HEADS:(i + 1) * N_HEADS, :], 0.0)
            o_ref[0, i:i + 1, :] = jnp.sum(blk, axis=0, keepdims=True).astype(o_ref.dtype)


def _attn_sample(q, k_new, v_new, cache_kt, cache_vt, page_table, bias2):
    bsz, t_new, _ = q.shape
    n_pages = page_table.shape[1]
    npg = SAMPLE_PAGES_PER_STEP
    steps = n_pages // npg
    rows = t_new * N_HEADS
    bias_col = jnp.tile(bias2, t_new).reshape(rows, 1)
    tok = pl.BlockSpec((1, t_new, D_MODEL), lambda b, j, pt: (b, 0, 0))

    def page_spec(r):
        return pl.BlockSpec((1, D_MODEL, PAGE_SIZE), lambda b, j, pt: (pt[b, n_pages - 1 - j * npg - r], 0, 0))

    grid_spec = pltpu.PrefetchScalarGridSpec(
        num_scalar_prefetch=1,
        grid=(bsz, steps),
        in_specs=[tok, tok, tok, pl.BlockSpec((rows, 1), lambda b, j, pt: (0, 0))]
        + [page_spec(r) for r in range(npg)] + [page_spec(r) for r in range(npg)],
        out_specs=tok,
        scratch_shapes=[
            pltpu.VMEM((rows, D_MODEL), BF16),
            pltpu.VMEM((rows, D_MODEL), F32),
            pltpu.VMEM((rows, 1), F32),
        ],
    )
    return pl.pallas_call(
        _attn_sample_kernel,
        grid_spec=grid_spec,
        out_shape=jax.ShapeDtypeStruct((bsz, t_new, D_MODEL), BF16),
        compiler_params=_cparams("arbitrary", "arbitrary"),
        name="attn_sample",
    )(page_table, q, k_new, v_new, bias_col, *([cache_kt] * npg), *([cache_vt] * npg))


def _router_params(w_coarse, b_coarse, w_fine, b_fine):
    wf = jnp.transpose(w_fine, (1, 0, 2)).reshape(D_MODEL, N_EXPERTS)
    w = jnp.concatenate([wf, w_coarse, jnp.zeros((D_MODEL, LANES - N_EXPERTS - N_GROUPS), F32)], axis=1)
    b = jnp.concatenate([b_fine.reshape(-1), b_coarse, jnp.zeros((LANES - N_EXPERTS - N_GROUPS,), F32)])
    return w, b.reshape(1, LANES)


def _ffn(hn, res, mw, g_final, tm, tm_rows, tm_moe, final_norm, sorted_buf=None):
    w_r, b_r, wg, wu, wd, layer = mw
    n = hn.shape[0]
    idx, gate, cnt = _router(hn, w_r, b_r, tm)
    n_tiles = pl.cdiv(TOP_K * n, tm_moe) + N_EXPERTS
    pos, tile_expert = _route_plan(idx, cnt, tm_rows, tm_moe, n_tiles)
    if sorted_buf is None:
        sorted_buf = jnp.zeros((n_tiles * tm_moe, D_MODEL), F32)
    xs = _dispatch(hn, pos, sorted_buf, tm_rows)
    ys = _moe_sorted(xs, tile_expert, wg, wu, wd, layer, tm_moe)
    return _combine(ys, pos, res, gate, g_final, tm_rows, final_norm), ys


def kernel(x_prompt, x_sample, state_conv, cache_k, cache_v, page_table, norm_mix, norm_ffn, norm_final,
           conv_w_pw1, conv_b_pw1, conv_w_dw, conv_b_dw, conv_ln_g, conv_ln_b, conv_w_pw2, conv_b_pw2,
           sb_w_qkv, sb_w_o, sb_logit_bias, moe_w_coarse, moe_b_coarse, moe_w_fine, moe_b_fine,
           moe_w_gate, moe_w_up, moe_w_down):
    bsz, t_len, _ = x_prompt.shape
    dbsz, t_new, _ = x_sample.shape
    n_p = bsz * t_len
    n_s = dbsz * t_new
    tm_p, tm_s = 512, n_s
    tmm_p, tmm_s = 256, 128
    tmr_p, tmr_s = 1024, n_s
    vec = lambda a: a.reshape(1, -1)
    g_fin = vec(norm_final)

    xp = x_prompt.reshape(n_p, D_MODEL)
    xs = x_sample.reshape(n_s, D_MODEL)

    moe_w = []
    for i in range(2):
        w_r, b_r = _router_params(moe_w_coarse[i], moe_b_coarse[i], moe_w_fine[i], moe_b_fine[i])
        moe_w.append((w_r, b_r, moe_w_gate, moe_w_up, moe_w_down, i))

    w1 = conv_w_pw1[0].astype(BF16)
    w2 = conv_w_pw2[0].astype(BF16)
    up = _pw1_glu(xp, vec(norm_mix[0]), w1, vec(conv_b_pw1[0]), tm_p)
    us = _pw1_glu(xs, vec(norm_mix[0]), w1, vec(conv_b_pw1[0]), tm_s)
    up3 = up.reshape(bsz, t_len, D_MODEL)
    full_s = jnp.concatenate([state_conv[0], us.reshape(dbsz, t_new, D_MODEL)], axis=1)
    conv_state_p = up3[:, t_len - (CONV_WIDTH - 1):, :][None]
    conv_state_s = full_s[:, t_new:, :][None]
    cp = _conv_prompt(up3, conv_w_dw[0], vec(conv_b_dw[0]), 256).reshape(n_p, D_MODEL)
    cs = _conv_sample(full_s, conv_w_dw[0], vec(conv_b_dw[0]), 8).reshape(n_s, D_MODEL)
    ln = (vec(conv_ln_g[0]), vec(conv_ln_b[0]), vec(norm_ffn[0]))
    yp, hp = _proj(cp, w2, vec(conv_b_pw2[0]), xp, *ln, tm_p, True)
    ys, hs = _proj(cs, w2, vec(conv_b_pw2[0]), xs, *ln, tm_s, True)
    yp, buf_p = _ffn(hp, yp, moe_w[0], g_fin, tm_p, tmr_p, tmm_p, False)
    ys, buf_s = _ffn(hs, ys, moe_w[0], g_fin, tm_s, tmr_s, tmm_s, False)

    wqkv = sb_w_qkv[0].astype(BF16)
    wo = sb_w_o[0].astype(BF16)
    bias2 = sb_logit_bias[0].astype(F32) * LOG2E
    qp, ktp, vtp, ktb, vtb = _qkv_t(yp.reshape(bsz, t_len, D_MODEL), vec(norm_mix[1]), wqkv[:, :D_MODEL],
                                    wqkv[:, D_MODEL:2 * D_MODEL].T, wqkv[:, 2 * D_MODEL:].T, tm_p)
    qs, ks, vs = _qkv(ys, vec(norm_mix[1]), wqkv, tm_s)
    op = _attn_prompt(qp, ktb, vtb, bias2)
    n_pool = cache_k.shape[1]
    cache_kt = jnp.transpose(cache_k[0], (0, 2, 3, 1)).reshape(n_pool, D_MODEL, PAGE_SIZE)
    cache_vt = jnp.transpose(cache_v[0], (0, 2, 3, 1)).reshape(n_pool, D_MODEL, PAGE_SIZE)
    os_ = _attn_sample(qs.reshape(dbsz, t_new, D_MODEL), ks.reshape(dbsz, t_new, D_MODEL),
                       vs.reshape(dbsz, t_new, D_MODEL), cache_kt, cache_vt,
                       page_table, bias2).reshape(n_s, D_MODEL)
    zero_b = jnp.zeros((1, D_MODEL), F32)
    nf1 = vec(norm_ffn[1])
    yp, hp = _proj(op, wo, zero_b, yp, nf1, nf1, nf1, tm_p, False)
    ys, hs = _proj(os_, wo, zero_b, ys, nf1, nf1, nf1, tm_s, False)
    yp, _ = _ffn(hp, yp, moe_w[1], g_fin, tm_p, tmr_p, tmm_p, True, buf_p)
    ys, _ = _ffn(hs, ys, moe_w[1], g_fin, tm_s, tmr_s, tmm_s, True, buf_s)

    def rows_p(xt):
        return jnp.transpose(xt.reshape(bsz, N_HEADS, HEAD_DIM, t_len), (0, 3, 1, 2))[None]

    hshape_s = (1, dbsz, t_new, N_HEADS, HEAD_DIM)
    return (yp.reshape(bsz, t_len, D_MODEL), ys.reshape(dbsz, t_new, D_MODEL), conv_state_p, conv_state_s,
            rows_p(ktp), rows_p(vtp), ks.reshape(hshape_s), vs.reshape(hshape_s))
```
